```python
import math, functools
import jax, jax.numpy as jnp
from jax import lax
import numpy as np

D_MODEL = 2048
BATCH = 4
SEQ = 4096
DEPTH = 4

GRID_W = 64
CTX_LEN = 256
N_MIXERS = 4
GW = D_MODEL // N_MIXERS
CONV_K = 4

LRU_BLOCKS = 8
LRU_BLOCK = GW // LRU_BLOCKS
LRU_C = 8.0

RET_HEADS = 4
RET_HEAD_DIM = GW // RET_HEADS
RET_CHUNK = 128
ROPE_BASE = 10000.0

SSD_HEAD_DIM = 64
SSD_HEADS = GW // SSD_HEAD_DIM
SSD_GROUPS = 2
SSD_STATE = 64
SSD_CHUNK = 128

GDN_HEADS = 4
GDN_HEAD_DIM = GW // GDN_HEADS
GDN_CHUNK = 64

A_SIZES = (GW, GW)
B_SIZES = (GW, GW, GW, GW)
C_SIZES = (GW, GW + 2 * SSD_GROUPS * SSD_STATE, SSD_HEADS)
D_SIZES = (3 * GW, GW, 2 * GDN_HEADS, 2 * GDN_HEADS)
GROUP_COLS = (sum(A_SIZES), sum(B_SIZES), sum(C_SIZES), sum(D_SIZES))
IN_COLS = sum(GROUP_COLS)

N_EXPERTS = 64
TOP_K = 8
N_EXPERT_GROUPS = 8
TOPK_GROUPS = 4
D_EXPERT = 256
ROUTED_SCALE = 2.5

DEEPNORM_ALPHA = (2 * DEPTH) ** 0.25
DEEPNORM_BETA = (8 * DEPTH) ** -0.25

kernel_name = 'hymba_style_bidir_lru_ret_ssd_gdn_moe'


def _split(t, sizes):
    idx = [int(s) for s in np.cumsum(sizes)[:-1]]
    return jnp.split(t, idx, axis=-1)


def _layer_norm(t, w, b, eps=1e-5):
    tf = t.astype(jnp.float32)
    mu = jnp.mean(tf, axis=-1, keepdims=True)
    var = jnp.mean(jnp.square(tf - mu), axis=-1, keepdims=True)
    return ((tf - mu) * lax.rsqrt(var + eps) * w + b).astype(t.dtype)


def _rms_norm(t, w, eps=1e-6):
    tf = t.astype(jnp.float32)
    return (tf * lax.rsqrt(jnp.mean(tf * tf, axis=-1, keepdims=True) + eps) * w).astype(t.dtype)


def _l2norm(t, eps=1e-6):
    return t * lax.rsqrt(jnp.sum(t * t, axis=-1, keepdims=True) + eps)


def _dw_conv(t, w, b=None):
    k, ch = w.shape
    left = k // 2
    y = lax.conv_general_dilated(t, w[:, None, :].astype(t.dtype), window_strides=(1,),
                                 padding=((left, k - 1 - left),),
                                 dimension_numbers=('NWC', 'WIO', 'NWC'),
                                 feature_group_count=ch)
    return y if b is None else y + b


def _rope_1d(t, pos):
    d = t.shape[-1]
    inv = ROPE_BASE ** (-jnp.arange(0, d, 2, dtype=jnp.float32) / d)
    ang = pos[:, None] * inv[None, :]
    cos = jnp.cos(ang)[None, :, None, :]
    sin = jnp.sin(ang)[None, :, None, :]
    t1, t2 = t[..., : d // 2], t[..., d // 2:]
    return jnp.concatenate([t1 * cos - t2 * sin, t1 * sin + t2 * cos], axis=-1)


def _rope_2d(t, rows, cols):
    h = t.shape[-1] // 2
    return jnp.concatenate([_rope_1d(t[..., :h], rows), _rope_1d(t[..., h:], cols)], axis=-1)


def _chunks(t, chunk):
    b, h, T = t.shape[:3]
    return jnp.moveaxis(t.reshape(b, h, T // chunk, chunk, *t.shape[3:]), 2, 0)


def _unchunk(y):
    n, b, h, c = y.shape[:4]
    return jnp.moveaxis(y, 0, 2).reshape(b, h, n * c, *y.shape[4:])


def _decay_scan(q, k, v, log_a, s0, chunk):
    tril = jnp.tril(jnp.ones((chunk, chunk), dtype=bool))

    def step(s, inp):
        qi, ki, vi, gi = inp
        cum = jnp.cumsum(gi, axis=-1)
        decay = jnp.exp(jnp.where(tril, cum[..., :, None] - cum[..., None, :], -jnp.inf))
        scores = jnp.einsum('bhik,bhjk->bhij', qi, ki) * decay
        y = (jnp.einsum('bhij,bhjv->bhiv', scores, vi)
             + jnp.einsum('bhik,bhkv->bhiv', qi * jnp.exp(cum)[..., None], s))
        last = cum[..., -1:]
        s = (jnp.exp(last)[..., None] * s
             + jnp.einsum('bhjk,bhjv->bhkv', ki * jnp.exp(last - cum)[..., None], vi))
        return s, y

    s_fin, y = lax.scan(step, s0, tuple(_chunks(t, chunk) for t in (q, k, v, log_a)))
    return _unchunk(y), s_fin


def _gated_delta_scan(q, k, v, log_a, beta, s0, chunk):
    dv = v.shape[-1]
    tril = jnp.tril(jnp.ones((chunk, chunk), dtype=bool))
    strict = jnp.tril(jnp.ones((chunk, chunk), dtype=bool), -1)
    eye = jnp.eye(chunk, dtype=q.dtype)

    def step(s, inp):
        qi, ki, vi, gi, bi = inp
        cum = jnp.cumsum(gi, axis=-1)
        dec = jnp.exp(jnp.where(tril, cum[..., :, None] - cum[..., None, :], -jnp.inf))
        kb = ki * bi[..., None]
        m = jnp.where(strict, jnp.einsum('bhik,bhjk->bhij', kb, ki) * dec, 0.0)
        rhs = jnp.concatenate([vi * bi[..., None], kb * jnp.exp(cum)[..., None]], axis=-1)
        sol = lax.linalg.triangular_solve(eye + m, rhs, left_side=True, lower=True,
                                          unit_diagonal=True)
        u, w = sol[..., :dv], sol[..., dv:]
        v_new = u - jnp.einsum('bhik,bhkv->bhiv', w, s)
        attn = jnp.einsum('bhik,bhjk->bhij', qi, ki) * dec
        o = (jnp.einsum('bhik,bhkv->bhiv', qi * jnp.exp(cum)[..., None], s)
             + jnp.einsum('bhij,bhjv->bhiv', attn, v_new))
        last = cum[..., -1:]
        s = (jnp.exp(last)[..., None] * s
             + jnp.einsum('bhjk,bhjv->bhkv', ki * jnp.exp(last - cum)[..., None], v_new))
        return s, o

    s_fin, o = lax.scan(step, s0, tuple(_chunks(t, chunk) for t in (q, k, v, log_a, beta)))
    return _unchunk(o), s_fin


def _lin_combine(l, r):
    al, bl = l
    ar, br = r
    return al * ar, ar * bl + br


def _lru_scan(a, b, h0):
    b = b.at[:, 0].add(a[:, 0] * h0)
    _, h = lax.associative_scan(_lin_combine, (a, b), axis=1)
    return h, h[:, -1]


def _prefix_two_pass(scan_fn, ctx_in, lat_in, s0, axis, reverse):
    if reverse:
        ctx_in = tuple(jnp.flip(t, axis) for t in ctx_in)
        lat_in = tuple(jnp.flip(t, axis) for t in lat_in)
    y_ctx, s_ctx = scan_fn(*ctx_in, s0)
    y_lat, _ = scan_fn(*lat_in, s_ctx)
    if reverse:
        y_ctx, y_lat = jnp.flip(y_ctx, axis), jnp.flip(y_lat, axis)
    return y_ctx, y_lat


def _block_diag(t, w):
    bsz, T, ch = t.shape
    y = jnp.einsum('btgi,gij->btgj', t.reshape(bsz, T, LRU_BLOCKS, LRU_BLOCK), w)
    return y.reshape(bsz, T, ch)


def _lru_coeffs(xc, ga_w, ga_b, gx_w, gx_b, lam):
    r = jax.nn.sigmoid(_block_diag(xc, ga_w) + ga_b)
    i = jax.nn.sigmoid(_block_diag(xc, gx_w) + gx_b)
    log_a = -LRU_C * r * jax.nn.softplus(-lam)
    return (jnp.exp(log_a), jnp.sqrt(-jnp.expm1(2.0 * log_a)) * (i * xc))


def _rglru_mixer(p_ctx, p_lat, conv_w, conv_b, ga_w, ga_b, gx_w, gx_b, lam):
    out_dtype = p_lat.dtype

    def prep(p):
        xb, gate = jnp.split(p, 2, axis=-1)
        return _dw_conv(xb, conv_w, conv_b).astype(jnp.float32), gate

    xc_c, gate_c = prep(p_ctx)
    xc_l, gate_l = prep(p_lat)
    h0 = jnp.zeros((p_lat.shape[0], GW), jnp.float32)
    h_c = h_l = 0.0
    for d in range(2):
        y_c, y_l = _prefix_two_pass(
            _lru_scan,
            _lru_coeffs(xc_c, ga_w[d], ga_b[d], gx_w[d], gx_b[d], lam[d]),
            _lru_coeffs(xc_l, ga_w[d], ga_b[d], gx_w[d], gx_b[d], lam[d]),
            h0, axis=1, reverse=(d == 1))
        h_c = h_c + y_c
        h_l = h_l + y_l
    out_c = jax.nn.gelu(gate_c.astype(jnp.float32)) * h_c
    out_l = jax.nn.gelu(gate_l.astype(jnp.float32)) * h_l
    return out_c.astype(out_dtype), out_l.astype(out_dtype)


def _retention_mixer(p_ctx, p_lat, rows, cols, log_decay, norm_w):
    out_dtype = p_lat.dtype

    def prep(p, rotate):
        bsz, t = p.shape[:2]
        q, k, v, g = jnp.split(p.astype(jnp.float32), 4, axis=-1)
        q, k, v = (z.reshape(bsz, t, RET_HEADS, RET_HEAD_DIM) for z in (q, k, v))
        if rotate:
            q, k = _rope_2d(q, rows, cols), _rope_2d(k, rows, cols)
        q = q * RET_HEAD_DIM ** -0.5
        return tuple(jnp.swapaxes(z, 1, 2) for z in (q, k, v)), g

    in_c, g_c = prep(p_ctx, False)
    in_l, g_l = prep(p_lat, True)
    s0 = jnp.zeros((p_lat.shape[0], RET_HEADS, RET_HEAD_DIM, RET_HEAD_DIM), jnp.float32)
    scan = functools.partial(_decay_scan, chunk=RET_CHUNK)
    o_c = o_l = 0.0
    for d in range(2):
        la = -jnp.exp(log_decay[d].astype(jnp.float32))[None, :, None]
        y_c, y_l = _prefix_two_pass(
            scan, (*in_c, jnp.broadcast_to(la, in_c[0].shape[:3])),
            (*in_l, jnp.broadcast_to(la, in_l[0].shape[:3])), s0, axis=2, reverse=(d == 1))
        o_c = o_c + y_c
        o_l = o_l + y_l

    def finish(o, g):
        o = jnp.swapaxes(o, 1, 2)
        o = _rms_norm(o, norm_w.reshape(RET_HEADS, RET_HEAD_DIM))
        return (jax.nn.silu(g) * o.reshape(*o.shape[:2], GW)).astype(out_dtype)

    return finish(o_c, g_c), finish(o_l, g_l)


def _ssd_mixer(p_ctx, p_lat, conv_w, conv_b, dt_bias, a_log, d_skip, norm_w):
    out_dtype = p_lat.dtype
    rep = SSD_HEADS // SSD_GROUPS

    def prep(p):
        bsz, t = p.shape[:2]
        z, xbc, dt_raw = _split(p, C_SIZES)
        xbc = jax.nn.silu(_dw_conv(xbc, conv_w, conv_b)).astype(jnp.float32)
        xs, bm, cm = _split(xbc, (GW, SSD_GROUPS * SSD_STATE, SSD_GROUPS * SSD_STATE))
        xs = xs.reshape(bsz, t, SSD_HEADS, SSD_HEAD_DIM).swapaxes(1, 2)
        bm = jnp.repeat(bm.reshape(bsz, t, SSD_GROUPS, SSD_STATE), rep, axis=2).swapaxes(1, 2)
        cm = jnp.repeat(cm.reshape(bsz, t, SSD_GROUPS, SSD_STATE), rep, axis=2).swapaxes(1, 2)
        return z, (xs, bm, cm, dt_raw.astype(jnp.float32).swapaxes(1, 2))

    def dir_inputs(xs, bm, cm, dt_raw, d):
        dt = jax.nn.softplus(dt_raw + dt_bias[d][None, :, None])
        log_a = -jnp.exp(a_log[d])[None, :, None] * dt
        return (cm, bm, xs * dt[..., None], log_a)

    z_c, in_c = prep(p_ctx)
    z_l, in_l = prep(p_lat)
    s0 = jnp.zeros((p_lat.shape[0], SSD_HEADS, SSD_STATE, SSD_HEAD_DIM), jnp.float32)
    scan = functools.partial(_decay_scan, chunk=SSD_CHUNK)
    y_c = d_skip[None, :, None, None] * in_c[0]
    y_l = d_skip[None, :, None, None] * in_l[0]
    for d in range(2):
        a_c, a_l = _prefix_two_pass(scan, dir_inputs(*in_c, d), dir_inputs(*in_l, d), s0,
                                    axis=2, reverse=(d == 1))
        y_c = y_c + a_c
        y_l = y_l + a_l

    def finish(y, z):
        y = jnp.swapaxes(y, 1, 2).reshape(z.shape[0], z.shape[1], GW)
        return _rms_norm(y * jax.nn.silu(z.astype(jnp.float32)), norm_w).astype(out_dtype)

    return finish(y_c, z_c), finish(y_l, z_l)


def _gdn_mixer(p_ctx, p_lat, conv_w, dt_bias, a_log, norm_w):
    out_dtype = p_lat.dtype

    def prep(p):
        bsz, t = p.shape[:2]
        qkv, z, a, b = _split(p, D_SIZES)
        qkv = jax.nn.silu(_dw_conv(qkv, conv_w)).astype(jnp.float32)
        q, k, v = (u.reshape(bsz, t, GDN_HEADS, GDN_HEAD_DIM).swapaxes(1, 2)
                   for u in jnp.split(qkv, 3, axis=-1))
        q = _l2norm(q) * GDN_HEAD_DIM ** -0.5
        k = _l2norm(k)
        a = a.astype(jnp.float32).reshape(bsz, t, 2, GDN_HEADS).transpose(2, 0, 3, 1)
        b = b.astype(jnp.float32).reshape(bsz, t, 2, GDN_HEADS).transpose(2, 0, 3, 1)
        return z, (q, k, v, a, b)

    def dir_inputs(q, k, v, a, b, d):
        log_a = -jnp.exp(a_log[d])[None, :, None] * jax.nn.softplus(a[d] + dt_bias[d][None, :, None])
        return (q, k, v, log_a, jax.nn.sigmoid(b[d]))

    z_c, in_c = prep(p_ctx)
    z_l, in_l = prep(p_lat)
    s0 = jnp.zeros((p_lat.shape[0], GDN_HEADS, GDN_HEAD_DIM, GDN_HEAD_DIM), jnp.float32)
    scan = functools.partial(_gated_delta_scan, chunk=GDN_CHUNK)
    o_c = o_l = 0.0
    for d in range(2):
        y_c, y_l = _prefix_two_pass(scan, dir_inputs(*in_c, d), dir_inputs(*in_l, d), s0,
                                    axis=2, reverse=(d == 1))
        o_c = o_c + y_c
        o_l = o_l + y_l

    def finish(o, z):
        bsz, t = z.shape[:2]
        o = _rms_norm(jnp.swapaxes(o, 1, 2), norm_w)
        gz = jax.nn.silu(z.astype(jnp.float32)).reshape(bsz, t, GDN_HEADS, GDN_HEAD_DIM)
        return (o * gz).reshape(bsz, t, GW).astype(out_dtype)

    return finish(o_c, z_c), finish(o_l, z_l)


def _moe_ffn(u, router_w, router_bias, w1, w3, w2, sw1, sw3, sw2):
    shp = u.shape
    t = u.reshape(-1, D_MODEL)
    n = t.shape[0]
    scores = jax.nn.sigmoid((t @ router_w).astype(jnp.float32))
    sel = scores + router_bias.astype(jnp.float32)
    grp = sel.reshape(n, N_EXPERT_GROUPS, N_EXPERTS // N_EXPERT_GROUPS)
    grp_score = jnp.sum(lax.top_k(grp, 2)[0], axis=-1)
    _, top_groups = lax.top_k(grp_score, TOPK_GROUPS)
    gmask = jnp.sum(jax.nn.one_hot(top_groups, N_EXPERT_GROUPS, dtype=jnp.float32), axis=1)
    emask = jnp.repeat(gmask, N_EXPERTS // N_EXPERT_GROUPS, axis=1)
    _, top_idx = lax.top_k(jnp.where(emask > 0, sel, -jnp.inf), TOP_K)
    wts = jnp.take_along_axis(scores, top_idx, axis=1)
    wts = wts / jnp.sum(wts, axis=-1, keepdims=True) * ROUTED_SCALE
    gate = jnp.sum(jax.nn.one_hot(top_idx, N_EXPERTS, dtype=jnp.float32) * wts[..., None], axis=1)
    h = jax.nn.silu(jnp.einsum('nd,edf->nef', t, w1)) * jnp.einsum('nd,edf->nef', t, w3)
    routed = jnp.einsum('nef,efd->nd', h * gate[..., None].astype(h.dtype), w2)
    shared = (jax.nn.silu(t @ sw1) * (t @ sw3)) @ sw2
    return (routed + shared).reshape(shp)


def setup_inputs(seed: int = 0) -> dict:
    key = jax.random.key(seed)
    ks = iter(jax.random.split(key, 48))
    L, D = DEPTH, D_MODEL

    def nrm(shape, scale):
        return jax.random.normal(next(ks), shape, jnp.float32) * scale

    def unif(shape, lo, hi):
        return jax.random.uniform(next(ks), shape, jnp.float32, lo, hi)

    inp = {}
    inp['x'] = nrm((BATCH, SEQ, D), 1.0)
    inp['c'] = nrm((BATCH, D), 1.0)
    inp['ctx'] = nrm((BATCH, CTX_LEN, D), 1.0)
    inp['c_ctx'] = nrm((D,), 1.0)
    inp['w_ada'] = nrm((L, D, 6 * D), 0.5 * D ** -0.5)
    inp['b_ada'] = nrm((L, 6 * D), 0.01)
    inp['w_in'] = nrm((L, D, IN_COLS), D ** -0.5)
    inp['lru_conv_w'] = nrm((L, CONV_K, GW), CONV_K ** -0.5)
    inp['lru_conv_b'] = nrm((L, GW), 0.01)
    inp['lru_gate_a_w'] = nrm((L, 2, LRU_BLOCKS, LRU_BLOCK, LRU_BLOCK), LRU_BLOCK ** -0.5)
    inp['lru_gate_a_b'] = nrm((L, 2, GW), 0.01)
    inp['lru_gate_x_w'] = nrm((L, 2, LRU_BLOCKS, LRU_BLOCK, LRU_BLOCK), LRU_BLOCK ** -0.5)
    inp['lru_gate_x_b'] = nrm((L, 2, GW), 0.01)
    s = unif((L, 2, GW), 0.9, 0.999) ** (1.0 / LRU_C)
    inp['lru_lambda'] = jnp.log(s) - jnp.log1p(-s)
    base = jnp.log(-jnp.log1p(-(2.0 ** (-5.0 - jnp.arange(RET_HEADS, dtype=jnp.float32)))))
    inp['ret_log_decay'] = base + nrm((L, 2, RET_HEADS), 0.05)
    inp['ret_norm_w'] = 1.0 + nrm((L, GW), 0.05)
    ssd_xbc = GW + 2 * SSD_GROUPS * SSD_STATE
    inp['ssd_conv_w'] = nrm((L, CONV_K, ssd_xbc), CONV_K ** -0.5)
    inp['ssd_conv_b'] = nrm((L, ssd_xbc), 0.01)
    dt = jnp.exp(unif((L, 2, SSD_HEADS), math.log(1e-3), math.log(1e-1)))
    inp['ssd_dt_bias'] = dt + jnp.log(-jnp.expm1(-dt))
    inp['ssd_a_log'] = jnp.log(unif((L, 2, SSD_HEADS), 1.0, 16.0))
    inp['ssd_d'] = 1.0 + nrm((L, SSD_HEADS), 0.1)
    inp['ssd_norm_w'] = 1.0 + nrm((L, GW), 0.05)
    inp['gdn_conv_w'] = nrm((L, CONV_K, 3 * GW), CONV_K ** -0.5)
    dtg = jnp.exp(unif((L, 2, GDN_HEADS), math.log(1e-3), math.log(1e-1)))
    inp['gdn_dt_bias'] = dtg + jnp.log(-jnp.expm1(-dtg))
    inp['gdn_a_log'] = jnp.log(unif((L, 2, GDN_HEADS), 1.0, 16.0))
    inp['gdn_norm_w'] = 1.0 + nrm((L, GDN_HEAD_DIM), 0.05)
    inp['w_out'] = nrm((L, D, D), DEEPNORM_BETA * D ** -0.5)
    inp['ln1_w'] = 1.0 + nrm((L, D), 0.05)
    inp['ln1_b'] = nrm((L, D), 0.01)
    inp['router_w'] = nrm((L, D, N_EXPERTS), D ** -0.5)
    inp['router_bias'] = nrm((L, N_EXPERTS), 0.01)
    inp['exp_w1'] = nrm((L, N_EXPERTS, D, D_EXPERT), D ** -0.5)
    inp['exp_w3'] = nrm((L, N_EXPERTS, D, D_EXPERT), D ** -0.5)
    inp['exp_w2'] = nrm((L, N_EXPERTS, D_EXPERT, D), DEEPNORM_BETA * D_EXPERT ** -0.5)
    inp['sh_w1'] = nrm((L, D, D_EXPERT), D ** -0.5)
    inp['sh_w3'] = nrm((L, D, D_EXPERT), D ** -0.5)
    inp['sh_w2'] = nrm((L, D_EXPERT, D), DEEPNORM_BETA * D_EXPERT ** -0.5)
    inp['ln2_w'] = 1.0 + nrm((L, D), 0.05)
    inp['ln2_b'] = nrm((L, D), 0.01)
    return inp


def reference(x, c, ctx, c_ctx, w_ada, b_ada, w_in, lru_conv_w, lru_conv_b, lru_gate_a_w,
              lru_gate_a_b, lru_gate_x_w, lru_gate_x_b, lru_lambda, ret_log_decay, ret_norm_w,
              ssd_conv_w, ssd_conv_b, ssd_dt_bias, ssd_a_log, ssd_d, ssd_norm_w, gdn_conv_w,
              gdn_dt_bias, gdn_a_log, gdn_norm_w, w_out, ln1_w, ln1_b, router_w, router_bias,
              exp_w1, exp_w3, exp_w2, sh_w1, sh_w3, sh_w2, ln2_w, ln2_b):
    T = x.shape[1]
    ROWS = T // GRID_W
    rows = jnp.repeat(jnp.arange(ROWS, dtype=jnp.float32), GRID_W)
    cols = jnp.tile(jnp.arange(GRID_W, dtype=jnp.float32), ROWS)
    silu_c = jax.nn.silu(c)
    silu_cc = jax.nn.silu(c_ctx)
    x_lat, x_ctx = x, ctx
    for l in range(DEPTH):
        mod_lat = (silu_c @ w_ada[l] + b_ada[l])[:, None, :]
        mod_ctx = (silu_cc @ w_ada[l] + b_ada[l])[None, None, :]
        sh1, sc1, g1, sh2, sc2, g2 = jnp.split(mod_lat, 6, axis=-1)
        csh1, csc1, cg1, csh2, csc2, cg2 = jnp.split(mod_ctx, 6, axis=-1)

        p_lat = (x_lat * (1.0 + sc1) + sh1) @ w_in[l]
        p_ctx = (x_ctx * (1.0 + csc1) + csh1) @ w_in[l]
        pa_l, pb_l, pc_l, pd_l = _split(p_lat, GROUP_COLS)
        pa_c, pb_c, pc_c, pd_c = _split(p_ctx, GROUP_COLS)

        ya_c, ya_l = _rglru_mixer(pa_c, pa_l, lru_conv_w[l], lru_conv_b[l], lru_gate_a_w[l],
                                  lru_gate_a_b[l], lru_gate_x_w[l], lru_gate_x_b[l], lru_lambda[l])
        yb_c, yb_l = _retention_mixer(pb_c, pb_l, rows, cols, ret_log_decay[l], ret_norm_w[l])
        yc_c, yc_l = _ssd_mixer(pc_c, pc_l, ssd_conv_w[l], ssd_conv_b[l], ssd_dt_bias[l],
                                ssd_a_log[l], ssd_d[l], ssd_norm_w[l])
        yd_c, yd_l = _gdn_mixer(pd_c, pd_l, gdn_conv_w[l], gdn_dt_bias[l], gdn_a_log[l], gdn_norm_w[l])

        y_lat = jnp.concatenate([ya_l, yb_l, yc_l, yd_l], axis=-1) @ w_out[l]
        x_lat = _layer_norm(DEEPNORM_ALPHA * x_lat + g1 * y_lat, ln1_w[l], ln1_b[l])
        f_lat = _moe_ffn(x_lat * (1.0 + sc2) + sh2, router_w[l], router_bias[l], exp_w1[l],
                         exp_w3[l], exp_w2[l], sh_w1[l], sh_w3[l], sh_w2[l])
        x_lat = _layer_norm(DEEPNORM_ALPHA * x_lat + g2 * f_lat, ln2_w[l], ln2_b[l])

        if l < DEPTH - 1:
            y_ctx = jnp.concatenate([ya_c, yb_c, yc_c, yd_c], axis=-1) @ w_out[l]
            x_ctx = _layer_norm(DEEPNORM_ALPHA * x_ctx + cg1 * y_ctx, ln1_w[l], ln1_b[l])
            f_ctx = _moe_ffn(x_ctx * (1.0 + csc2) + csh2, router_w[l], router_bias[l], exp_w1[l],
                             exp_w3[l], exp_w2[l], sh_w1[l], sh_w3[l], sh_w2[l])
            x_ctx = _layer_norm(DEEPNORM_ALPHA * x_ctx + cg2 * f_ctx, ln2_w[l], ln2_b[l])
    return x_lat
```

```python
import functools
import math

import jax
import jax.numpy as jnp
from jax import lax
from jax.experimental import pallas as pl
from jax.experimental.pallas import tpu as pltpu

f32 = jnp.float32
bf16 = jnp.bfloat16
HIGHEST = lax.Precision.HIGHEST

D_MODEL = 2048
GW = 512
GRID_W = 64
CONV_K = 4
LRU_C = 8.0
RET_HEADS = 4
HEAD_DIM = 128
ROPE_BASE = 10000.0
SSD_HEADS = 8
SSD_HEAD_DIM = 64
SSD_STATE = 64
GDN_HEADS = 4
N_EXPERTS = 64
TOP_K = 8
N_EXPERT_GROUPS = 8
TOPK_GROUPS = 4
D_EXPERT = 256
ROUTED_SCALE = 2.5

ROW_TILE = 256
CHUNK = 128
IN_COLS_PAD = 6528
CV_COLS = 2816
VMEM_LIMIT = 56 * 1024 * 1024

P_XB, P_GATE, P_Q, P_K, P_V, P_G, P_CZ, P_DZ, P_DQKV, P_XBC, P_SMALL = (
    0, 512, 1024, 1536, 2048, 2560, 3072, 3584, 4096, 5632, 6400)


def _cparams(sem):
    return pltpu.CompilerParams(dimension_semantics=sem, vmem_limit_bytes=VMEM_LIMIT)


def _sigmoid(x):
    return 1.0 / (1.0 + jnp.exp(-x))


def _silu(x):
    return x * _sigmoid(x)


def _softplus(x):
    return jnp.maximum(x, 0.0) + jnp.log(1.0 + jnp.exp(-jnp.abs(x)))


def _dot(a, b):
    return jnp.dot(a.astype(bf16), b.astype(bf16), preferred_element_type=f32)


def _dot_nt(a, b):
    return lax.dot_general(a.astype(bf16), b.astype(bf16), (((1,), (1,)), ((), ())),
                           preferred_element_type=f32)


def _dot_tn(a, b):
    return lax.dot_general(a.astype(bf16), b.astype(bf16), (((0,), (0,)), ((), ())),
                           preferred_element_type=f32)


def _chunk_order(i, n_ctx, n_all, reverse):
    if not reverse:
        return i
    return jnp.where(i < n_ctx, n_ctx - 1 - i, n_all + n_ctx - 1 - i)


def _ada_body(s_ref, w_ref, b_ref, o_ref):
    x = s_ref[...]
    o_ref[0] = jnp.dot(_silu(x), w_ref[0], precision=HIGHEST, preferred_element_type=f32) + b_ref[0]


def _ada_call(svec, w_ada, b_ada):
    nl, d, d6 = w_ada.shape
    tn = 1024
    return pl.pallas_call(
        _ada_body,
        out_shape=jax.ShapeDtypeStruct((nl, 8, d6), f32),
        grid=(nl, d6 // tn),
        in_specs=[pl.BlockSpec((8, d), lambda l, j: (0, 0)),
                  pl.BlockSpec((1, d, tn), lambda l, j: (l, 0, j)),
                  pl.BlockSpec((1, 1, tn), lambda l, j: (l, 0, j))],
        out_specs=pl.BlockSpec((1, 8, tn), lambda l, j: (l, 0, j)),
        compiler_params=_cparams(("arbitrary", "arbitrary")),
        name="ada",
    )(svec, w_ada, b_ada.reshape(nl, 1, d6))


def _inproj_body(x_ref, sc_ref, sh_ref, w_ref, o_ref, *, sub):
    for s in range(sub):
        rows = slice(s * ROW_TILE, (s + 1) * ROW_TILE)
        xm = x_ref[rows, :] * (1.0 + sc_ref[s]) + sh_ref[s]
        o_ref[rows, :] = _dot(xm, w_ref[...])


def _inproj_call(x, sc, sh, w_bf):
    n, d = x.shape
    sub = 2
    tm = sub * ROW_TILE
    tn = IN_COLS_PAD // 3
    return pl.pallas_call(
        functools.partial(_inproj_body, sub=sub),
        out_shape=jax.ShapeDtypeStruct((n, IN_COLS_PAD), f32),
        grid=(3, n // tm),
        in_specs=[pl.BlockSpec((tm, d), lambda j, i: (i, 0)),
                  pl.BlockSpec((sub, 1, d), lambda j, i: (i, 0, 0)),
                  pl.BlockSpec((sub, 1, d), lambda j, i: (i, 0, 0)),
                  pl.BlockSpec((d, tn), lambda j, i: (0, j))],
        out_specs=pl.BlockSpec((tm, tn), lambda j, i: (i, j)),
        compiler_params=_cparams(("arbitrary", "arbitrary")),
        name="inproj",
    )(x, sc, sh, w_bf)


def _conv_body(x_ref, w_ref, b_ref, o_ref, *, seq, lc):
    j = pl.program_id(1)
    w = w_ref[...]
    bias = b_ref[...]
    use_act = jnp.logical_or(j < 6, j >= 8)
    row = lax.broadcasted_iota(jnp.int32, (CHUNK, 1), 0)

    def chunk(c, carry):
        r0 = pl.multiple_of(c * CHUNK, CHUNK)
        cur = x_ref[pl.ds(r0, CHUNK), :]
        prev = x_ref[pl.ds(pl.multiple_of(jnp.maximum(r0 - 8, 0), 8), 8), :]
        nxt = x_ref[pl.ds(pl.multiple_of(jnp.minimum(r0 + CHUNK, seq - 8), 8), 8), :]
        ext = jnp.concatenate([prev, cur, nxt], axis=0)
        t = r0 + row
        s0 = jnp.where(r0 < lc, 0, lc)
        s1 = jnp.where(r0 < lc, lc, seq)
        acc = jnp.zeros_like(cur) + bias
        for k in range(CONV_K):
            o = k - CONV_K // 2
            seg = ext[8 + o:8 + o + CHUNK]
            valid = jnp.logical_and(t + o >= s0, t + o < s1)
            acc = acc + jnp.where(valid, seg, 0.0) * w[k:k + 1, :]
        o_ref[pl.ds(r0, CHUNK), :] = jnp.where(use_act, _silu(acc), acc)
        return carry

    lax.fori_loop(0, seq // CHUNK, chunk, 0)


def _conv_call(p, wcat, bcat, nb, seq, lc):
    n = p.shape[0]
    cb = 256

    def in_map(b, j):
        return (b, jnp.where(j < 6, P_DQKV // cb + j, jnp.where(j < 8, j - 6, P_XBC // cb - 8 + j)))

    return pl.pallas_call(
        functools.partial(_conv_body, seq=seq, lc=lc),
        out_shape=jax.ShapeDtypeStruct((n, CV_COLS), f32),
        grid=(nb, CV_COLS // cb),
        in_specs=[pl.BlockSpec((seq, cb), in_map),
                  pl.BlockSpec((CONV_K, cb), lambda b, j: (0, j)),
                  pl.BlockSpec((1, cb), lambda b, j: (0, j))],
        out_specs=pl.BlockSpec((seq, cb), lambda b, j: (b, j)),
        compiler_params=_cparams(("arbitrary", "arbitrary")),
        name="dwconv",
    )(p, wcat, bcat)


def _lru_body(xc_ref, wg_ref, bg_ref, sp_ref, o_ref, a_s, b_s, h_s, *, reverse):
    i = pl.program_id(1)

    @pl.when(i == 0)
    def _():
        h_s[...] = jnp.zeros_like(h_s)

    xc = xc_ref[...]
    gates = _dot(xc, wg_ref[0]) + bg_ref[0]
    r = _sigmoid(gates[:, :GW])
    ig = _sigmoid(gates[:, GW:])
    a = jnp.exp(-LRU_C * r * sp_ref[0])
    a_s[...] = a
    b_s[...] = jnp.sqrt(1.0 - a * a) * (ig * xc)

    def step(jj, h):
        t = (ROW_TILE - 1 - jj) if reverse else jj
        h = a_s[pl.ds(t, 1), :] * h + b_s[pl.ds(t, 1), :]
        o_ref[pl.ds(t, 1), :] = h
        return h

    h_s[...] = lax.fori_loop(0, ROW_TILE, step, h_s[...], unroll=8)


def _lru_call(cv, wg, bg, sp, nb, seq, lc, d, reverse):
    n = cv.shape[0]
    nt = seq // ROW_TILE
    nctx = lc // ROW_TILE

    def rows(b, i):
        return b * nt + _chunk_order(i, nctx, nt, reverse)

    return pl.pallas_call(
        functools.partial(_lru_body, reverse=reverse),
        out_shape=jax.ShapeDtypeStruct((n, GW), f32),
        grid=(nb, nt),
        in_specs=[pl.BlockSpec((ROW_TILE, GW), lambda b, i: (rows(b, i), 3)),
                  pl.BlockSpec((1, GW, 2 * GW), lambda b, i: (d, 0, 0)),
                  pl.BlockSpec((1, 1, 2 * GW), lambda b, i: (d, 0, 0)),
                  pl.BlockSpec((1, 1, GW), lambda b, i: (d, 0, 0))],
        out_specs=pl.BlockSpec((ROW_TILE, GW), lambda b, i: (rows(b, i), 0)),
        scratch_shapes=[pltpu.VMEM((ROW_TILE, GW), f32), pltpu.VMEM((ROW_TILE, GW), f32),
                        pltpu.VMEM((1, GW), f32)],
        compiler_params=_cparams(("arbitrary", "arbitrary")),
        name="lru_rev" if reverse else "lru_fwd",
    )(cv, wg, bg, sp)


def _tri_masks(reverse):
    ii = lax.broadcasted_iota(jnp.int32, (CHUNK, CHUNK), 0)
    jj = lax.broadcasted_iota(jnp.int32, (CHUNK, CHUNK), 1)
    if reverse:
        return jj >= ii, jj > ii, ii, jj
    return ii >= jj, ii > jj, ii, jj


def _rope(x, cos, sin):
    lane = lax.broadcasted_iota(jnp.int32, x.shape, 1)
    swapped = jnp.where(lane % 64 < 32, pltpu.roll(x, 96, 1), pltpu.roll(x, 32, 1))
    return x * cos + swapped * sin


def _ret_body(lg_ref, q_ref, k_ref, v_ref, cos_ref, sin_ref, o_ref, s_s, *, reverse):
    i = pl.program_id(1)

    @pl.when(i == 0)
    def _():
        s_s[...] = jnp.zeros_like(s_s)

    mask, _, ii, jj = _tri_masks(reverse)
    diff = ((jj - ii) if reverse else (ii - jj)).astype(f32)
    col = lax.broadcasted_iota(jnp.int32, (CHUNK, 1), 0).astype(f32)
    cos = cos_ref[...]
    sin = sin_ref[...]
    for h in range(RET_HEADS):
        lg = lg_ref[h]
        lanes = slice(h * HEAD_DIM, (h + 1) * HEAD_DIM)
        qh = _rope(q_ref[:, lanes], cos, sin) * (HEAD_DIM ** -0.5)
        kh = _rope(k_ref[:, lanes], cos, sin)
        vh = v_ref[:, lanes]
        dec = jnp.where(mask, jnp.exp(lg * diff), 0.0)
        if reverse:
            eq = jnp.exp(lg * (CHUNK - col))
            ek = jnp.exp(lg * col)
        else:
            eq = jnp.exp(lg * (col + 1.0))
            ek = jnp.exp(lg * (CHUNK - 1.0 - col))
        st = s_s[h]
        y = _dot(_dot_nt(qh, kh) * dec, vh) + _dot(qh * eq, st)
        s_s[h] = jnp.exp(lg * CHUNK) * st + _dot_tn(kh * ek, vh)
        o_ref[:, lanes] = y


def _ret_call(p, lg, cos, sin, nb, seq, lc, reverse):
    n = p.shape[0]
    nc = seq // CHUNK
    nctx = lc // CHUNK

    def ch(i):
        return _chunk_order(i, nctx, nc, reverse)

    def pspec(col):
        return pl.BlockSpec((CHUNK, GW), lambda b, i: (b * nc + ch(i), col // GW))

    return pl.pallas_call(
        functools.partial(_ret_body, reverse=reverse),
        out_shape=jax.ShapeDtypeStruct((n, GW), f32),
        grid=(nb, nc),
        in_specs=[pl.BlockSpec(memory_space=pltpu.SMEM),
                  pspec(P_Q), pspec(P_K), pspec(P_V),
                  pl.BlockSpec((CHUNK, HEAD_DIM), lambda b, i: (ch(i), 0)),
                  pl.BlockSpec((CHUNK, HEAD_DIM), lambda b, i: (ch(i), 0))],
        out_specs=pl.BlockSpec((CHUNK, GW), lambda b, i: (b * nc + ch(i), 0)),
        scratch_shapes=[pltpu.VMEM((RET_HEADS, HEAD_DIM, HEAD_DIM), f32)],
        compiler_params=_cparams(("arbitrary", "arbitrary")),
        name="ret_rev" if reverse else "ret_fwd",
    )(lg, p, p, p, cos, sin)


def _cumsums(la_col, la_row, reverse):
    mask, _, ii, jj = _tri_masks(reverse)
    m_col = jnp.where(mask, 1.0, 0.0)
    m_row = jnp.where(jnp.logical_not(mask) | (ii == jj), 1.0, 0.0)
    cum_col = jnp.dot(m_col, la_col, precision=HIGHEST, preferred_element_type=f32)
    cum_row = jnp.dot(la_row, m_row, precision=HIGHEST, preferred_element_type=f32)
    return cum_col, cum_row, mask


def _ssd_body(xs_ref, bc_ref, sm_ref, smt_ref, pc_ref, pr_ref, o_ref, s_s, *, reverse):
    i = pl.program_id(1)

    @pl.when(i == 0)
    def _():
        s_s[...] = jnp.zeros_like(s_s)

    dt_col = _softplus(sm_ref[...] + pc_ref[0, 0:1, :])
    la_col = pc_ref[0, 1:2, :] * dt_col
    dt_row = _softplus(smt_ref[...] + pr_ref[0, :, 0:1])
    la_row = pr_ref[0, :, 1:2] * dt_row
    cum_col, cum_row, mask = _cumsums(la_col, la_row, reverse)
    tot = cum_col[0:1, :] if reverse else cum_col[CHUNK - 1:CHUNK, :]
    bm = bc_ref[:, :128]
    cm = bc_ref[:, 128:]
    lane = lax.broadcasted_iota(jnp.int32, (1, 128), 1)
    for g in range(2):
        gmask = (lane // SSD_STATE) == g
        cmg = jnp.where(gmask, cm, 0.0)
        bmg = jnp.where(gmask, bm, 0.0)
        s = _dot_nt(cmg, bmg)
        for pp in range(2 * g, 2 * g + 2):
            lanes = slice(pp * 128, (pp + 1) * 128)
            xpair = xs_ref[:, lanes]
            st = s_s[pp]
            y = jnp.zeros((CHUNK, 128), f32)
            new_s = jnp.zeros((128, 128), f32)
            for hh in range(2):
                h = 2 * pp + hh
                hmask = (lane // SSD_HEAD_DIM) == hh
                cc = cum_col[:, h:h + 1]
                cr = cum_row[h:h + 1, :]
                dec = jnp.where(mask, jnp.exp(cc - cr), 0.0)
                vh = jnp.where(hmask, xpair * dt_col[:, h:h + 1], 0.0)
                sth = jnp.where(hmask, st, 0.0)
                th = tot[:, h:h + 1]
                y = y + _dot(s * dec, vh) + _dot(cmg * jnp.exp(cc), sth)
                new_s = new_s + jnp.exp(th) * sth + _dot_tn(bmg * jnp.exp(th - cc), vh)
            s_s[pp] = new_s
            o_ref[:, lanes] = y


def _ssd_call(cv, p, smt, pc, pr, nb, seq, lc, d, reverse):
    n = cv.shape[0]
    nc = seq // CHUNK
    nctx = lc // CHUNK

    def row(b, i):
        return b * nc + _chunk_order(i, nctx, nc, reverse)

    return pl.pallas_call(
        functools.partial(_ssd_body, reverse=reverse),
        out_shape=jax.ShapeDtypeStruct((n, GW), f32),
        grid=(nb, nc),
        in_specs=[pl.BlockSpec((CHUNK, GW), lambda b, i: (row(b, i), 4)),
                  pl.BlockSpec((CHUNK, 256), lambda b, i: (row(b, i), 10)),
                  pl.BlockSpec((CHUNK, 128), lambda b, i: (row(b, i), P_SMALL // 128)),
                  pl.BlockSpec((8, CHUNK), lambda b, i: (0, row(b, i))),
                  pl.BlockSpec((1, 2, 128), lambda b, i: (d, 0, 0)),
                  pl.BlockSpec((1, 8, 2), lambda b, i: (d, 0, 0))],
        out_specs=pl.BlockSpec((CHUNK, GW), lambda b, i: (row(b, i), 0)),
        scratch_shapes=[pltpu.VMEM((4, 128, 128), f32)],
        compiler_params=_cparams(("arbitrary", "arbitrary")),
        name="ssd_rev" if reverse else "ssd_fwd",
    )(cv, cv, p, smt, pc, pr)


TRI_BASE = 8


def _unit_tri_inverse_minus_eye(m, ii, jj):
    base = (ii // TRI_BASE) == (jj // TRI_BASE)
    pw = jnp.where(base, -m, 0.0)
    acc = pw
    size = 2
    while size < TRI_BASE:
        pw = _dot(pw, pw)
        acc = acc + pw + _dot(acc, pw)
        size *= 2
    half = TRI_BASE
    while half < CHUNK:
        off = jnp.logical_and((ii // (2 * half)) == (jj // (2 * half)), (ii // half) != (jj // half))
        mo = jnp.where(off, m, 0.0)
        x = mo + _dot(acc, mo)
        acc = acc - x - _dot(x, acc)
        half *= 2
    return acc


def _gdn_body(q_ref, k_ref, v_ref, sm_ref, smt_ref, pc_ref, pr_ref, o_ref, s_s, *, reverse, d):
    i = pl.program_id(1)

    @pl.when(i == 0)
    def _():
        s_s[...] = jnp.zeros_like(s_s)

    sm = sm_ref[...]
    smt = smt_ref[...]
    la_col = pc_ref[0, 1:2, :] * _softplus(sm + pc_ref[0, 0:1, :])
    la_row = pr_ref[0, :, 1:2] * _softplus(smt + pr_ref[0, :, 0:1])
    beta_col = _sigmoid(sm)
    cum_col, cum_row, mask = _cumsums(la_col, la_row, reverse)
    _, strict, blk_i, blk_j = _tri_masks(reverse)
    tot = cum_col[0:1, :] if reverse else cum_col[CHUNK - 1:CHUNK, :]
    for h in range(GDN_HEADS):
        ca = 8 + d * GDN_HEADS + h
        cb = 16 + d * GDN_HEADS + h
        lanes = slice(h * HEAD_DIM, (h + 1) * HEAD_DIM)
        qh = q_ref[:, lanes]
        kh = k_ref[:, lanes]
        vh = v_ref[:, lanes]
        qn = qh * lax.rsqrt(jnp.sum(qh * qh, axis=-1, keepdims=True) + 1e-6) * (HEAD_DIM ** -0.5)
        kn = kh * lax.rsqrt(jnp.sum(kh * kh, axis=-1, keepdims=True) + 1e-6)
        cc = cum_col[:, ca:ca + 1]
        cr = cum_row[ca:ca + 1, :]
        beta = beta_col[:, cb:cb + 1]
        dec = jnp.where(mask, jnp.exp(cc - cr), 0.0)
        kb = kn * beta
        m = jnp.where(strict, _dot_nt(kb, kn) * dec, 0.0)
        acc = _unit_tri_inverse_minus_eye(m, blk_i, blk_j)
        rhs = jnp.concatenate([vh * beta, kb * jnp.exp(cc)], axis=1)
        sol = rhs + _dot(acc, rhs)
        u = sol[:, :HEAD_DIM]
        w = sol[:, HEAD_DIM:]
        st = s_s[h]
        v_new = u - _dot(w, st)
        attn = _dot_nt(qn, kn) * dec
        o_ref[:, lanes] = _dot(qn * jnp.exp(cc), st) + _dot(attn, v_new)
        th = tot[:, ca:ca + 1]
        s_s[h] = jnp.exp(th) * st + _dot_tn(kn * jnp.exp(th - cc), v_new)


def _gdn_call(cv, p, smt, pc, pr, nb, seq, lc, d, reverse):
    n = cv.shape[0]
    nc = seq // CHUNK
    nctx = lc // CHUNK

    def row(b, i):
        return b * nc + _chunk_order(i, nctx, nc, reverse)

    def cspec(col):
        return pl.BlockSpec((CHUNK, GW), lambda b, i: (row(b, i), col))

    return pl.pallas_call(
        functools.partial(_gdn_body, reverse=reverse, d=d),
        out_shape=jax.ShapeDtypeStruct((n, GW), f32),
        grid=(nb, nc),
        in_specs=[cspec(0), cspec(1), cspec(2),
                  pl.BlockSpec((CHUNK, 128), lambda b, i: (row(b, i), P_SMALL // 128)),
                  pl.BlockSpec((32, CHUNK), lambda b, i: (0, row(b, i))),
                  pl.BlockSpec((1, 2, 128), lambda b, i: (d, 0, 0)),
                  pl.BlockSpec((1, 32, 2), lambda b, i: (d, 0, 0))],
        out_specs=pl.BlockSpec((CHUNK, GW), lambda b, i: (row(b, i), 0)),
        scratch_shapes=[pltpu.VMEM((GDN_HEADS, HEAD_DIM, HEAD_DIM), f32)],
        compiler_params=_cparams(("arbitrary", "arbitrary")),
        name="gdn_rev" if reverse else "gdn_fwd",
    )(cv, cv, cv, p, smt, pc, pr)


def _rms_lanes(x, w, eps):
    return x * lax.rsqrt(jnp.mean(x * x, axis=-1, keepdims=True) + eps) * w


def _finish_body(hf, hb, rf, rb, sf, sb, gf, gb, gate, rg, cz, dz, xs, retw, ssdd, ssdw, gdnw, o_ref):
    g = gate[...]
    gelu = 0.5 * g * (1.0 + jnp.tanh(math.sqrt(2.0 / math.pi) * (g + 0.044715 * (g * g * g))))
    o_ref[:, 0:GW] = (gelu * (hf[...] + hb[...])).astype(bf16)

    ro = rf[...] + rb[...]
    rgate = _silu(rg[...])
    go = gf[...] + gb[...]
    ggate = _silu(dz[...])
    for h in range(RET_HEADS):
        lanes = slice(h * HEAD_DIM, (h + 1) * HEAD_DIM)
        o_ref[:, GW + h * HEAD_DIM:GW + (h + 1) * HEAD_DIM] = (
            rgate[:, lanes] * _rms_lanes(ro[:, lanes], retw[:, lanes], 1e-6)).astype(bf16)
        o_ref[:, 3 * GW + h * HEAD_DIM:3 * GW + (h + 1) * HEAD_DIM] = (
            _rms_lanes(go[:, lanes], gdnw[:, lanes], 1e-6) * ggate[:, lanes]).astype(bf16)

    sy = (sf[...] + sb[...] + ssdd[...] * xs[...]) * _silu(cz[...])
    o_ref[:, 2 * GW:3 * GW] = _rms_lanes(sy, ssdw[...], 1e-6).astype(bf16)


def _finish_call(scans, p, cv, retw, ssdd, ssdw, gdnw):
    n = p.shape[0]
    tm = ROW_TILE

    def rowspec(col):
        return pl.BlockSpec((tm, GW), lambda i: (i, col))

    vec = pl.BlockSpec((1, GW), lambda i: (0, 0))
    return pl.pallas_call(
        _finish_body,
        out_shape=jax.ShapeDtypeStruct((n, D_MODEL), bf16),
        grid=(n // tm,),
        in_specs=[rowspec(0)] * 8 + [rowspec(P_GATE // GW), rowspec(P_G // GW), rowspec(P_CZ // GW),
                                     rowspec(P_DZ // GW), rowspec(4), vec, vec, vec, vec],
        out_specs=pl.BlockSpec((tm, D_MODEL), lambda i: (i, 0)),
        compiler_params=_cparams(("arbitrary",)),
        name="mixer_finish",
    )(*scans, p, p, p, p, cv, retw, ssdd, ssdw, gdnw)


def _layer_norm_rows(t, w, b):
    mu = jnp.mean(t, axis=-1, keepdims=True)
    tc = t - mu
    var = jnp.mean(tc * tc, axis=-1, keepdims=True)
    return tc * lax.rsqrt(var + 1e-5) * w + b


def _outproj_body(y_ref, w_ref, x_ref, g1_ref, lw_ref, lb_ref, sc_ref, sh_ref, rw_ref,
                  x1_ref, u_ref, lg_ref, *, alpha):
    y = jnp.dot(y_ref[...], w_ref[...], preferred_element_type=f32)
    x1 = _layer_norm_rows(alpha * x_ref[...] + g1_ref[0] * y, lw_ref[...], lb_ref[...])
    x1_ref[...] = x1
    u = x1 * (1.0 + sc_ref[0]) + sh_ref[0]
    u_ref[...] = u.astype(bf16)
    lg_ref[...] = lax.dot_general(rw_ref[...], u, (((1,), (1,)), ((), ())),
                                  precision=HIGHEST, preferred_element_type=f32)


def _outproj_call(ycat, w_bf, x, g1, lw, lb, sc2, sh2, rwt, alpha):
    n, d = x.shape
    tm = ROW_TILE
    mod = pl.BlockSpec((1, 1, d), lambda i: (i, 0, 0))
    vec = pl.BlockSpec((1, d), lambda i: (0, 0))
    return pl.pallas_call(
        functools.partial(_outproj_body, alpha=alpha),
        out_shape=(jax.ShapeDtypeStruct((n, d), f32), jax.ShapeDtypeStruct((n, d), bf16),
                   jax.ShapeDtypeStruct((N_EXPERTS, n), f32)),
        grid=(n // tm,),
        in_specs=[pl.BlockSpec((tm, d), lambda i: (i, 0)),
                  pl.BlockSpec((d, d), lambda i: (0, 0)),
                  pl.BlockSpec((tm, d), lambda i: (i, 0)),
                  mod, vec, vec, mod, mod,
                  pl.BlockSpec((N_EXPERTS, d), lambda i: (0, 0))],
        out_specs=(pl.BlockSpec((tm, d), lambda i: (i, 0)), pl.BlockSpec((tm, d), lambda i: (i, 0)),
                   pl.BlockSpec((N_EXPERTS, tm), lambda i: (0, i))),
        compiler_params=_cparams(("arbitrary",)),
        name="outproj_ln1",
    )(ycat, w_bf, x, g1, lw, lb, sc2, sh2, rwt)


def _route_body(lg_ref, bias_ref, gate_ref, idx_ref, wt_ref):
    tn = lg_ref.shape[1]
    scores = _sigmoid(lg_ref[...])
    sel = scores + bias_ref[...]
    gsz = N_EXPERTS // N_EXPERT_GROUPS
    neg = -jnp.inf
    sel3 = sel.reshape(N_EXPERT_GROUPS, gsz, tn)
    io3 = lax.broadcasted_iota(jnp.int32, sel3.shape, 1)
    m1 = jnp.max(sel3, axis=1)
    first = jnp.min(jnp.where(sel3 == m1[:, None, :], io3, gsz), axis=1)
    m2 = jnp.max(jnp.where(io3 == first[:, None, :], neg, sel3), axis=1)
    gscore = m1 + m2
    iog = lax.broadcasted_iota(jnp.int32, gscore.shape, 0)
    gsel = None
    for _ in range(TOPK_GROUPS):
        gm = jnp.max(gscore, axis=0, keepdims=True)
        gi = jnp.min(jnp.where(gscore == gm, iog, N_EXPERT_GROUPS), axis=0, keepdims=True)
        hit = iog == gi
        gsel = hit if gsel is None else jnp.logical_or(gsel, hit)
        gscore = jnp.where(hit, neg, gscore)
    emask = jnp.broadcast_to(gsel[:, None, :], sel3.shape).reshape(N_EXPERTS, tn)
    cand = jnp.where(emask, sel, neg)
    ioe = lax.broadcasted_iota(jnp.int32, cand.shape, 0)
    chosen = None
    idxs = []
    wts = []
    for _ in range(TOP_K):
        cm = jnp.max(cand, axis=0, keepdims=True)
        ci = jnp.min(jnp.where(cand == cm, ioe, N_EXPERTS), axis=0, keepdims=True)
        hit = ioe == ci
        chosen = hit if chosen is None else jnp.logical_or(chosen, hit)
        idxs.append(ci)
        wts.append(jnp.sum(jnp.where(hit, scores, 0.0), axis=0, keepdims=True))
        cand = jnp.where(hit, neg, cand)
    wsum = wts[0]
    for w in wts[1:]:
        wsum = wsum + w
    idx_ref[...] = jnp.concatenate(idxs, axis=0)
    wt_ref[...] = jnp.concatenate([w / wsum * ROUTED_SCALE for w in wts], axis=0)
    gate_t = jnp.where(chosen, scores / wsum * ROUTED_SCALE, 0.0)
    gate_ref[...] = jnp.concatenate([gate_t, jnp.zeros_like(gate_t)], axis=0).T


def _route_call(logits_t, bias):
    n = logits_t.shape[1]
    tn = ROW_TILE
    return pl.pallas_call(
        _route_body,
        out_shape=(jax.ShapeDtypeStruct((n, 2 * N_EXPERTS), f32),
                   jax.ShapeDtypeStruct((TOP_K, n), jnp.int32),
                   jax.ShapeDtypeStruct((TOP_K, n), f32)),
        grid=(n // tn,),
        in_specs=[pl.BlockSpec((N_EXPERTS, tn), lambda i: (0, i)),
                  pl.BlockSpec((N_EXPERTS, 1), lambda i: (0, 0))],
        out_specs=(pl.BlockSpec((tn, 2 * N_EXPERTS), lambda i: (i, 0)),
                   pl.BlockSpec((TOP_K, tn), lambda i: (0, i)),
                   pl.BlockSpec((TOP_K, tn), lambda i: (0, i))),
        compiler_params=_cparams(("arbitrary",)),
        name="router",
    )(logits_t, bias)


def _moe_body(u_ref, gate_ref, w1_ref, w3_ref, w2_ref, s1_ref, s3_ref, s2_ref, o_ref):
    e = pl.program_id(1)
    u = u_ref[...]

    @pl.when(e == 0)
    def _():
        hs = _silu(jnp.dot(u, s1_ref[...], preferred_element_type=f32)) * jnp.dot(
            u, s3_ref[...], preferred_element_type=f32)
        o_ref[...] = _dot(hs, s2_ref[...])

    lane = lax.broadcasted_iota(jnp.int32, gate_ref.shape, 1)
    g = jnp.sum(jnp.where(lane == e, gate_ref[...], 0.0), axis=-1, keepdims=True)
    h1 = jnp.dot(u, w1_ref[0].astype(bf16), preferred_element_type=f32)
    h3 = jnp.dot(u, w3_ref[0].astype(bf16), preferred_element_type=f32)
    h = _silu(h1) * h3 * g
    o_ref[...] += _dot(h, w2_ref[0])


def _moe_call(u_bf, gate, w1, w3, w2, s1, s3, s2):
    n, d = u_bf.shape
    tm = next(t for t in (1024, 512, 256) if n % t == 0)
    ne, _, fe = w1.shape
    return pl.pallas_call(
        _moe_body,
        out_shape=jax.ShapeDtypeStruct((n, d), f32),
        grid=(n // tm, ne),
        in_specs=[pl.BlockSpec((tm, d), lambda i, e: (i, 0)),
                  pl.BlockSpec((tm, 2 * N_EXPERTS), lambda i, e: (i, 0)),
                  pl.BlockSpec((1, d, fe), lambda i, e: (e, 0, 0)),
                  pl.BlockSpec((1, d, fe), lambda i, e: (e, 0, 0)),
                  pl.BlockSpec((1, fe, d), lambda i, e: (e, 0, 0)),
                  pl.BlockSpec((d, fe), lambda i, e: (0, 0)),
                  pl.BlockSpec((d, fe), lambda i, e: (0, 0)),
                  pl.BlockSpec((fe, d), lambda i, e: (0, 0))],
        out_specs=pl.BlockSpec((tm, d), lambda i, e: (i, 0)),
        compiler_params=_cparams(("arbitrary", "arbitrary")),
        name="moe_dense",
    )(u_bf, gate, w1, w3, w2, s1, s3, s2)


def _ln2_body(x_ref, f_ref, g2_ref, lw_ref, lb_ref, o_ref, *, alpha):
    o_ref[...] = _layer_norm_rows(alpha * x_ref[...] + g2_ref[0] * f_ref[...], lw_ref[...], lb_ref[...])


def _ln2_call(x1, f, g2, lw, lb, alpha):
    n, d = x1.shape
    tm = ROW_TILE
    row = pl.BlockSpec((tm, d), lambda i: (i, 0))
    return pl.pallas_call(
        functools.partial(_ln2_body, alpha=alpha),
        out_shape=jax.ShapeDtypeStruct((n, d), f32),
        grid=(n // tm,),
        in_specs=[row, row, pl.BlockSpec((1, 1, d), lambda i: (i, 0, 0)),
                  pl.BlockSpec((1, d), lambda i: (0, 0)), pl.BlockSpec((1, d), lambda i: (0, 0))],
        out_specs=row,
        compiler_params=_cparams(("arbitrary",)),
        name="ln2",
    )(x1, f, g2, lw, lb)


def _block_diag(w):
    g, n, _ = w.shape
    eye = jnp.eye(g, dtype=w.dtype)
    return (eye[:, None, :, None] * w[:, :, None, :]).reshape(g * n, g * n)


def _rope_tables(seq, lc):
    half = HEAD_DIM // 2
    t = jnp.arange(seq - lc, dtype=f32)
    inv = ROPE_BASE ** (-jnp.arange(0, half, 2, dtype=f32) / half)
    ar = jnp.floor(t / GRID_W)[:, None] * inv[None, :]
    ac = (t - jnp.floor(t / GRID_W) * GRID_W)[:, None] * inv[None, :]
    cos = jnp.concatenate([jnp.cos(ar), jnp.cos(ar), jnp.cos(ac), jnp.cos(ac)], axis=1)
    sin = jnp.concatenate([-jnp.sin(ar), jnp.sin(ar), -jnp.sin(ac), jnp.sin(ac)], axis=1)
    cos = jnp.concatenate([jnp.ones((lc, HEAD_DIM), f32), cos], axis=0)
    sin = jnp.concatenate([jnp.zeros((lc, HEAD_DIM), f32), sin], axis=0)
    return cos, sin


def _pad_lanes(v, offset, width=128):
    return jnp.zeros((width,), f32).at[offset:offset + v.shape[0]].set(v)


def kernel(x, c, ctx, c_ctx, w_ada, b_ada, w_in, lru_conv_w, lru_conv_b, lru_gate_a_w, lru_gate_a_b,
           lru_gate_x_w, lru_gate_x_b, lru_lambda, ret_log_decay, ret_norm_w, ssd_conv_w, ssd_conv_b,
           ssd_dt_bias, ssd_a_log, ssd_d, ssd_norm_w, gdn_conv_w, gdn_dt_bias, gdn_a_log, gdn_norm_w,
           w_out, ln1_w, ln1_b, router_w, router_bias, exp_w1, exp_w3, exp_w2, sh_w1, sh_w3, sh_w2,
           ln2_w, ln2_b):
    nb, lat, d = x.shape
    lc = ctx.shape[1]
    depth = w_ada.shape[0]
    seq = lc + lat
    n = nb * seq
    ntile = n // ROW_TILE
    tiles_per_seq = seq // ROW_TILE
    alpha = (2 * depth) ** 0.25
    assert d == D_MODEL and lc == ROW_TILE and lat % ROW_TILE == 0 and nb <= 7

    xall = jnp.concatenate([ctx, x], axis=1).reshape(n, d)

    svec = jnp.zeros((8, d), f32).at[0].set(c_ctx).at[1:1 + nb].set(c)
    modtab = _ada_call(svec, w_ada, b_ada).reshape(depth, 8, 6, d)
    tile = jnp.arange(ntile)
    tile_row = jnp.where(tile % tiles_per_seq == 0, 0, 1 + tile // tiles_per_seq)
    cos, sin = _rope_tables(seq, lc)

    for l in range(depth):
        mods = [modtab[l, :, k, :][tile_row][:, None, :] for k in range(6)]
        sh1, sc1, g1, sh2, sc2, g2 = mods

        wl = w_in[l]
        w_re = jnp.concatenate([wl[:, :3584], wl[:, 5896:6408], wl[:, 4360:5896], wl[:, 3584:4352],
                                wl[:, 4352:4360], wl[:, 6408:6424],
                                jnp.zeros((d, IN_COLS_PAD - 6424), f32)], axis=1).astype(bf16)
        p = _inproj_call(xall, sc1, sh1, w_re)
        smt = p[:, P_SMALL:P_SMALL + 32].T

        wcat = jnp.concatenate([gdn_conv_w[l], lru_conv_w[l], ssd_conv_w[l]], axis=1)
        bcat = jnp.concatenate([jnp.zeros((3 * GW,), f32), lru_conv_b[l], ssd_conv_b[l]])[None, :]
        cv = _conv_call(p, wcat, bcat, nb, seq, lc)

        wg = jnp.stack([jnp.concatenate([_block_diag(lru_gate_a_w[l, dd]), _block_diag(lru_gate_x_w[l, dd])],
                                        axis=1) for dd in range(2)]).astype(bf16)
        bg = jnp.concatenate([lru_gate_a_b[l], lru_gate_x_b[l]], axis=1)[:, None, :]
        sp = jax.nn.softplus(-lru_lambda[l])[:, None, :]
        hs = [_lru_call(cv, wg, bg, sp, nb, seq, lc, dd, bool(dd)) for dd in range(2)]

        lg = -jnp.exp(ret_log_decay[l].astype(f32))
        rs = [_ret_call(p, lg[dd], cos, sin, nb, seq, lc, bool(dd)) for dd in range(2)]

        a_ssd = -jnp.exp(ssd_a_log[l])
        pc = jnp.stack([jnp.stack([_pad_lanes(ssd_dt_bias[l, dd], 0), _pad_lanes(a_ssd[dd], 0)])
                        for dd in range(2)])
        pr = jnp.stack([jnp.stack([ssd_dt_bias[l, dd], a_ssd[dd]], axis=1) for dd in range(2)])
        ss = [_ssd_call(cv, p, smt, pc, pr, nb, seq, lc, dd, bool(dd)) for dd in range(2)]

        a_gdn = -jnp.exp(gdn_a_log[l])
        pcg = jnp.stack([jnp.stack([_pad_lanes(gdn_dt_bias[l, dd], 8 + 4 * dd),
                                    _pad_lanes(a_gdn[dd], 8 + 4 * dd)]) for dd in range(2)])
        prg = jnp.stack([jnp.stack([_pad_lanes(gdn_dt_bias[l, dd], 8 + 4 * dd, 32),
                                    _pad_lanes(a_gdn[dd], 8 + 4 * dd, 32)], axis=1) for dd in range(2)])
        gs = [_gdn_call(cv, p, smt, pcg, prg, nb, seq, lc, dd, bool(dd)) for dd in range(2)]

        ycat = _finish_call(hs + rs + ss + gs, p, cv, ret_norm_w[l][None, :],
                            jnp.repeat(ssd_d[l], SSD_HEAD_DIM)[None, :], ssd_norm_w[l][None, :],
                            jnp.tile(gdn_norm_w[l], GDN_HEADS)[None, :])

        x1, u_bf, logits_t = _outproj_call(ycat, w_out[l].astype(bf16), xall, g1, ln1_w[l][None, :],
                                           ln1_b[l][None, :], sc2, sh2, router_w[l].T, alpha)
        gate, _, _ = _route_call(logits_t, router_bias[l][:, None])
        f = _moe_call(u_bf, gate, exp_w1[l], exp_w3[l], exp_w2[l], sh_w1[l].astype(bf16),
                      sh_w3[l].astype(bf16), sh_w2[l].astype(bf16))
        xall = _ln2_call(x1, f, g2, ln2_w[l][None, :], ln2_b[l][None, :], alpha)

    return xall.reshape(nb, seq, d)[:, lc:, :]
```

```python
import functools
import math

import jax
import jax.numpy as jnp
from jax import lax
from jax.experimental import pallas as pl
from jax.experimental.pallas import tpu as pltpu

f32 = jnp.float32
bf16 = jnp.bfloat16
HIGHEST = lax.Precision.HIGHEST

D_MODEL = 2048
GW = 512
GRID_W = 64
CONV_K = 4
LRU_C = 8.0
RET_HEADS = 4
HEAD_DIM = 128
ROPE_BASE = 10000.0
SSD_HEADS = 8
SSD_HEAD_DIM = 64
SSD_STATE = 64
GDN_HEADS = 4
N_EXPERTS = 64
TOP_K = 8
N_EXPERT_GROUPS = 8
TOPK_GROUPS = 4
D_EXPERT = 256
ROUTED_SCALE = 2.5

ROW_TILE = 256
CHUNK = 128
IN_COLS_PAD = 6528
CV_COLS = 2816
VMEM_LIMIT = 56 * 1024 * 1024

P_XB, P_GATE, P_Q, P_K, P_V, P_G, P_CZ, P_DZ, P_DQKV, P_XBC, P_SMALL = (
    0, 512, 1024, 1536, 2048, 2560, 3072, 3584, 4096, 5632, 6400)


def _cparams(sem):
    return pltpu.CompilerParams(dimension_semantics=sem, vmem_limit_bytes=VMEM_LIMIT)


def _sigmoid(x):
    return 1.0 / (1.0 + jnp.exp(-x))


def _silu(x):
    return x * _sigmoid(x)


def _softplus(x):
    return jnp.maximum(x, 0.0) + jnp.log(1.0 + jnp.exp(-jnp.abs(x)))


def _dot(a, b):
    return jnp.dot(a.astype(bf16), b.astype(bf16), preferred_element_type=f32)


def _dot_nt(a, b):
    return lax.dot_general(a.astype(bf16), b.astype(bf16), (((1,), (1,)), ((), ())),
                           preferred_element_type=f32)


def _dot_tn(a, b):
    return lax.dot_general(a.astype(bf16), b.astype(bf16), (((0,), (0,)), ((), ())),
                           preferred_element_type=f32)


def _chunk_order(i, n_ctx, n_all, reverse):
    if not reverse:
        return i
    return jnp.where(i < n_ctx, n_ctx - 1 - i, n_all + n_ctx - 1 - i)


def _ada_body(s_ref, w_ref, b_ref, o_ref):
    x = s_ref[...]
    o_ref[0] = jnp.dot(_silu(x), w_ref[0], precision=HIGHEST, preferred_element_type=f32) + b_ref[0]


def _ada_call(svec, w_ada, b_ada):
    nl, d, d6 = w_ada.shape
    tn = 1024
    return pl.pallas_call(
        _ada_body,
        out_shape=jax.ShapeDtypeStruct((nl, 8, d6), f32),
        grid=(nl, d6 // tn),
        in_specs=[pl.BlockSpec((8, d), lambda l, j: (0, 0)),
                  pl.BlockSpec((1, d, tn), lambda l, j: (l, 0, j)),
                  pl.BlockSpec((1, 1, tn), lambda l, j: (l, 0, j))],
        out_specs=pl.BlockSpec((1, 8, tn), lambda l, j: (l, 0, j)),
        compiler_params=_cparams(("arbitrary", "arbitrary")),
        name="ada",
    )(svec, w_ada, b_ada.reshape(nl, 1, d6))


def _inproj_body(x_ref, sc_ref, sh_ref, w_ref, o_ref, *, sub):
    for s in range(sub):
        rows = slice(s * ROW_TILE, (s + 1) * ROW_TILE)
        xm = x_ref[rows, :] * (1.0 + sc_ref[s]) + sh_ref[s]
        o_ref[rows, :] = _dot(xm, w_ref[...])


def _inproj_call(x, sc, sh, w_bf):
    n, d = x.shape
    sub = 2
    tm = sub * ROW_TILE
    tn = IN_COLS_PAD // 3
    return pl.pallas_call(
        functools.partial(_inproj_body, sub=sub),
        out_shape=jax.ShapeDtypeStruct((n, IN_COLS_PAD), f32),
        grid=(3, n // tm),
        in_specs=[pl.BlockSpec((tm, d), lambda j, i: (i, 0)),
                  pl.BlockSpec((sub, 1, d), lambda j, i: (i, 0, 0)),
                  pl.BlockSpec((sub, 1, d), lambda j, i: (i, 0, 0)),
                  pl.BlockSpec((d, tn), lambda j, i: (0, j))],
        out_specs=pl.BlockSpec((tm, tn), lambda j, i: (i, j)),
        compiler_params=_cparams(("arbitrary", "arbitrary")),
        name="inproj",
    )(x, sc, sh, w_bf)


def _conv_body(x_ref, w_ref, b_ref, o_ref, *, seq, lc):
    j = pl.program_id(1)
    w = w_ref[...]
    bias = b_ref[...]
    use_act = jnp.logical_or(j < 6, j >= 8)
    row = lax.broadcasted_iota(jnp.int32, (CHUNK, 1), 0)

    def chunk(c, carry):
        r0 = pl.multiple_of(c * CHUNK, CHUNK)
        cur = x_ref[pl.ds(r0, CHUNK), :]
        prev = x_ref[pl.ds(pl.multiple_of(jnp.maximum(r0 - 8, 0), 8), 8), :]
        nxt = x_ref[pl.ds(pl.multiple_of(jnp.minimum(r0 + CHUNK, seq - 8), 8), 8), :]
        ext = jnp.concatenate([prev, cur, nxt], axis=0)
        t = r0 + row
        s0 = jnp.where(r0 < lc, 0, lc)
        s1 = jnp.where(r0 < lc, lc, seq)
        acc = jnp.zeros_like(cur) + bias
        for k in range(CONV_K):
            o = k - CONV_K // 2
            seg = ext[8 + o:8 + o + CHUNK]
            valid = jnp.logical_and(t + o >= s0, t + o < s1)
            acc = acc + jnp.where(valid, seg, 0.0) * w[k:k + 1, :]
        o_ref[pl.ds(r0, CHUNK), :] = jnp.where(use_act, _silu(acc), acc)
        return carry

    lax.fori_loop(0, seq // CHUNK, chunk, 0)


def _conv_call(p, wcat, bcat, nb, seq, lc):
    n = p.shape[0]
    cb = 256

    def in_map(b, j):
        return (b, jnp.where(j < 6, P_DQKV // cb + j, jnp.where(j < 8, j - 6, P_XBC // cb - 8 + j)))

    return pl.pallas_call(
        functools.partial(_conv_body, seq=seq, lc=lc),
        out_shape=jax.ShapeDtypeStruct((n, CV_COLS), f32),
        grid=(nb, CV_COLS // cb),
        in_specs=[pl.BlockSpec((seq, cb), in_map),
                  pl.BlockSpec((CONV_K, cb), lambda b, j: (0, j)),
                  pl.BlockSpec((1, cb), lambda b, j: (0, j))],
        out_specs=pl.BlockSpec((seq, cb), lambda b, j: (b, j)),
        compiler_params=_cparams(("arbitrary", "arbitrary")),
        name="dwconv",
    )(p, wcat, bcat)


def _lru_body(xc_ref, wg_ref, bg_ref, sp_ref, o_ref, a_s, b_s, h_s, *, reverse):
    i = pl.program_id(1)

    @pl.when(i == 0)
    def _():
        h_s[...] = jnp.zeros_like(h_s)

    xc = xc_ref[...]
    gates = _dot(xc, wg_ref[0]) + bg_ref[0]
    r = _sigmoid(gates[:, :GW])
    ig = _sigmoid(gates[:, GW:])
    a = jnp.exp(-LRU_C * r * sp_ref[0])
    a_s[...] = a
    b_s[...] = jnp.sqrt(1.0 - a * a) * (ig * xc)

    def step(jj, h):
        t = (ROW_TILE - 1 - jj) if reverse else jj
        h = a_s[pl.ds(t, 1), :] * h + b_s[pl.ds(t, 1), :]
        o_ref[pl.ds(t, 1), :] = h
        return h

    h_s[...] = lax.fori_loop(0, ROW_TILE, step, h_s[...], unroll=8)


def _lru_call(cv, wg, bg, sp, nb, seq, lc, d, reverse):
    n = cv.shape[0]
    nt = seq // ROW_TILE
    nctx = lc // ROW_TILE

    def rows(b, i):
        return b * nt + _chunk_order(i, nctx, nt, reverse)

    return pl.pallas_call(
        functools.partial(_lru_body, reverse=reverse),
        out_shape=jax.ShapeDtypeStruct((n, GW), f32),
        grid=(nb, nt),
        in_specs=[pl.BlockSpec((ROW_TILE, GW), lambda b, i: (rows(b, i), 3)),
                  pl.BlockSpec((1, GW, 2 * GW), lambda b, i: (d, 0, 0)),
                  pl.BlockSpec((1, 1, 2 * GW), lambda b, i: (d, 0, 0)),
                  pl.BlockSpec((1, 1, GW), lambda b, i: (d, 0, 0))],
        out_specs=pl.BlockSpec((ROW_TILE, GW), lambda b, i: (rows(b, i), 0)),
        scratch_shapes=[pltpu.VMEM((ROW_TILE, GW), f32), pltpu.VMEM((ROW_TILE, GW), f32),
                        pltpu.VMEM((1, GW), f32)],
        compiler_params=_cparams(("arbitrary", "arbitrary")),
        name="lru_rev" if reverse else "lru_fwd",
    )(cv, wg, bg, sp)


def _tri_masks(reverse):
    ii = lax.broadcasted_iota(jnp.int32, (CHUNK, CHUNK), 0)
    jj = lax.broadcasted_iota(jnp.int32, (CHUNK, CHUNK), 1)
    if reverse:
        return jj >= ii, jj > ii, ii, jj
    return ii >= jj, ii > jj, ii, jj


def _rope(x, cos, sin):
    lane = lax.broadcasted_iota(jnp.int32, x.shape, 1)
    swapped = jnp.where(lane % 64 < 32, pltpu.roll(x, 96, 1), pltpu.roll(x, 32, 1))
    return x * cos + swapped * sin


def _ret_body(lg_ref, q_ref, k_ref, v_ref, cos_ref, sin_ref, o_ref, s_s, *, reverse):
    i = pl.program_id(1)

    @pl.when(i == 0)
    def _():
        s_s[...] = jnp.zeros_like(s_s)

    mask, _, ii, jj = _tri_masks(reverse)
    diff = ((jj - ii) if reverse else (ii - jj)).astype(f32)
    col = lax.broadcasted_iota(jnp.int32, (CHUNK, 1), 0).astype(f32)
    cos = cos_ref[...]
    sin = sin_ref[...]
    for h in range(RET_HEADS):
        lg = lg_ref[h]
        lanes = slice(h * HEAD_DIM, (h + 1) * HEAD_DIM)
        qh = _rope(q_ref[:, lanes], cos, sin) * (HEAD_DIM ** -0.5)
        kh = _rope(k_ref[:, lanes], cos, sin)
        vh = v_ref[:, lanes]
        dec = jnp.where(mask, jnp.exp(lg * diff), 0.0)
        if reverse:
            eq = jnp.exp(lg * (CHUNK - col))
            ek = jnp.exp(lg * col)
        else:
            eq = jnp.exp(lg * (col + 1.0))
            ek = jnp.exp(lg * (CHUNK - 1.0 - col))
        st = s_s[h]
        y = _dot(_dot_nt(qh, kh) * dec, vh) + _dot(qh * eq, st)
        s_s[h] = jnp.exp(lg * CHUNK) * st + _dot_tn(kh * ek, vh)
        o_ref[:, lanes] = y


def _ret_call(p, lg, cos, sin, nb, seq, lc, reverse):
    n = p.shape[0]
    nc = seq // CHUNK
    nctx = lc // CHUNK

    def ch(i):
        return _chunk_order(i, nctx, nc, reverse)

    def pspec(col):
        return pl.BlockSpec((CHUNK, GW), lambda b, i: (b * nc + ch(i), col // GW))

    return pl.pallas_call(
        functools.partial(_ret_body, reverse=reverse),
        out_shape=jax.ShapeDtypeStruct((n, GW), f32),
        grid=(nb, nc),
        in_specs=[pl.BlockSpec(memory_space=pltpu.SMEM),
                  pspec(P_Q), pspec(P_K), pspec(P_V),
                  pl.BlockSpec((CHUNK, HEAD_DIM), lambda b, i: (ch(i), 0)),
                  pl.BlockSpec((CHUNK, HEAD_DIM), lambda b, i: (ch(i), 0))],
        out_specs=pl.BlockSpec((CHUNK, GW), lambda b, i: (b * nc + ch(i), 0)),
        scratch_shapes=[pltpu.VMEM((RET_HEADS, HEAD_DIM, HEAD_DIM), f32)],
        compiler_params=_cparams(("arbitrary", "arbitrary")),
        name="ret_rev" if reverse else "ret_fwd",
    )(lg, p, p, p, cos, sin)


def _cumsums(la_col, la_row, reverse):
    mask, _, ii, jj = _tri_masks(reverse)
    m_col = jnp.where(mask, 1.0, 0.0)
    m_row = jnp.where(jnp.logical_not(mask) | (ii == jj), 1.0, 0.0)
    cum_col = jnp.dot(m_col, la_col, precision=HIGHEST, preferred_element_type=f32)
    cum_row = jnp.dot(la_row, m_row, precision=HIGHEST, preferred_element_type=f32)
    return cum_col, cum_row, mask


def _ssd_body(xs_ref, bc_ref, sm_ref, smt_ref, pc_ref, pr_ref, o_ref, s_s, *, reverse):
    i = pl.program_id(1)

    @pl.when(i == 0)
    def _():
        s_s[...] = jnp.zeros_like(s_s)

    dt_col = _softplus(sm_ref[...] + pc_ref[0, 0:1, :])
    la_col = pc_ref[0, 1:2, :] * dt_col
    dt_row = _softplus(smt_ref[...] + pr_ref[0, :, 0:1])
    la_row = pr_ref[0, :, 1:2] * dt_row
    cum_col, cum_row, mask = _cumsums(la_col, la_row, reverse)
    tot = cum_col[0:1, :] if reverse else cum_col[CHUNK - 1:CHUNK, :]
    bm = bc_ref[:, :128]
    cm = bc_ref[:, 128:]
    lane = lax.broadcasted_iota(jnp.int32, (1, 128), 1)
    for g in range(2):
        gmask = (lane // SSD_STATE) == g
        cmg = jnp.where(gmask, cm, 0.0)
        bmg = jnp.where(gmask, bm, 0.0)
        s = _dot_nt(cmg, bmg)
        for pp in range(2 * g, 2 * g + 2):
            lanes = slice(pp * 128, (pp + 1) * 128)
            xpair = xs_ref[:, lanes]
            st = s_s[pp]
            y = jnp.zeros((CHUNK, 128), f32)
            new_s = jnp.zeros((128, 128), f32)
            for hh in range(2):
                h = 2 * pp + hh
                hmask = (lane // SSD_HEAD_DIM) == hh
                cc = cum_col[:, h:h + 1]
                cr = cum_row[h:h + 1, :]
                dec = jnp.where(mask, jnp.exp(cc - cr), 0.0)
                vh = jnp.where(hmask, xpair * dt_col[:, h:h + 1], 0.0)
                sth = jnp.where(hmask, st, 0.0)
                th = tot[:, h:h + 1]
                y = y + _dot(s * dec, vh) + _dot(cmg * jnp.exp(cc), sth)
                new_s = new_s + jnp.exp(th) * sth + _dot_tn(bmg * jnp.exp(th - cc), vh)
            s_s[pp] = new_s
            o_ref[:, lanes] = y


def _ssd_call(cv, p, smt, pc, pr, nb, seq, lc, d, reverse):
    n = cv.shape[0]
    nc = seq // CHUNK
    nctx = lc // CHUNK

    def row(b, i):
        return b * nc + _chunk_order(i, nctx, nc, reverse)

    return pl.pallas_call(
        functools.partial(_ssd_body, reverse=reverse),
        out_shape=jax.ShapeDtypeStruct((n, GW), f32),
        grid=(nb, nc),
        in_specs=[pl.BlockSpec((CHUNK, GW), lambda b, i: (row(b, i), 4)),
                  pl.BlockSpec((CHUNK, 256), lambda b, i: (row(b, i), 10)),
                  pl.BlockSpec((CHUNK, 128), lambda b, i: (row(b, i), P_SMALL // 128)),
                  pl.BlockSpec((8, CHUNK), lambda b, i: (0, row(b, i))),
                  pl.BlockSpec((1, 2, 128), lambda b, i: (d, 0, 0)),
                  pl.BlockSpec((1, 8, 2), lambda b, i: (d, 0, 0))],
        out_specs=pl.BlockSpec((CHUNK, GW), lambda b, i: (row(b, i), 0)),
        scratch_shapes=[pltpu.VMEM((4, 128, 128), f32)],
        compiler_params=_cparams(("arbitrary", "arbitrary")),
        name="ssd_rev" if reverse else "ssd_fwd",
    )(cv, cv, p, smt, pc, pr)


TRI_BASE = 8
GDN_GROUP = 16


def _each(fn, *lists):
    return [fn(*args) for args in zip(*lists)]


def _unit_tri_inverse_minus_eye(ms, ii, jj):
    base = (ii // TRI_BASE) == (jj // TRI_BASE)
    pws = _each(lambda m: jnp.where(base, -m, 0.0), ms)
    accs = pws
    size = 2
    while size < TRI_BASE:
        pws = _each(lambda p: _dot(p, p), pws)
        accs = _each(lambda a, p: a + p + _dot(a, p), accs, pws)
        size *= 2
    half = TRI_BASE
    while half < CHUNK:
        off = jnp.logical_and((ii // (2 * half)) == (jj // (2 * half)), (ii // half) != (jj // half))
        mos = _each(lambda m: jnp.where(off, m, 0.0), ms)
        xs = _each(lambda a, mo: mo + _dot(a, mo), accs, mos)
        accs = _each(lambda a, x: a - x - _dot(x, a), accs, xs)
        half *= 2
    return accs


def _gdn_body(q_ref, k_ref, v_ref, sm_ref, smt_ref, pc_ref, pr_ref, o_ref, s_s, *, reverse, d, nb, group):
    i = pl.program_id(0)

    @pl.when(i == 0)
    def _():
        s_s[...] = jnp.zeros_like(s_s)

    mask, strict, blk_i, blk_j = _tri_masks(reverse)
    per_b = []
    for b in range(nb):
        sm = sm_ref[b]
        la_col = pc_ref[0, 1:2, :] * _softplus(sm + pc_ref[0, 0:1, :])
        la_row = pr_ref[0, :, 1:2] * _softplus(smt_ref[b] + pr_ref[0, :, 0:1])
        cum_col, cum_row, _ = _cumsums(la_col, la_row, reverse)
        tot = cum_col[0:1, :] if reverse else cum_col[CHUNK - 1:CHUNK, :]
        per_b.append((cum_col, cum_row, tot, _sigmoid(sm)))

    chains = [(b, h) for b in range(nb) for h in range(GDN_HEADS)]
    for g0 in range(0, len(chains), group):
        grp = chains[g0:g0 + group]

        def lanes(h):
            return slice(h * HEAD_DIM, (h + 1) * HEAD_DIM)

        def l2n(x):
            return x * lax.rsqrt(jnp.sum(x * x, axis=-1, keepdims=True) + 1e-6)

        ca = [8 + d * GDN_HEADS + h for _, h in grp]
        cb = [16 + d * GDN_HEADS + h for _, h in grp]
        qn = [l2n(q_ref[b, :, lanes(h)]) * (HEAD_DIM ** -0.5) for b, h in grp]
        kn = [l2n(k_ref[b, :, lanes(h)]) for b, h in grp]
        vh = [v_ref[b, :, lanes(h)] for b, h in grp]
        cc = [per_b[b][0][:, c:c + 1] for (b, _), c in zip(grp, ca)]
        cr = [per_b[b][1][c:c + 1, :] for (b, _), c in zip(grp, ca)]
        th = [per_b[b][2][:, c:c + 1] for (b, _), c in zip(grp, ca)]
        beta = [per_b[b][3][:, c:c + 1] for (b, _), c in zip(grp, cb)]
        st = [s_s[b * GDN_HEADS + h] for b, h in grp]
        dec = _each(lambda c, r: jnp.where(mask, jnp.exp(c - r), 0.0), cc, cr)
        kb = _each(lambda k, bt: k * bt, kn, beta)
        ms = _each(lambda a, k, dc: jnp.where(strict, _dot_nt(a, k) * dc, 0.0), kb, kn, dec)
        attn = _each(lambda q, k, dc: _dot_nt(q, k) * dc, qn, kn, dec)
        qs = _each(lambda q, c, s: _dot(q * jnp.exp(c), s), qn, cc, st)
        accs = _unit_tri_inverse_minus_eye(ms, blk_i, blk_j)
        rhs = _each(lambda v, bt, k, c: jnp.concatenate([v * bt, k * jnp.exp(c)], axis=1), vh, beta, kb, cc)
        sol = _each(lambda r, a: r + _dot(a, r), rhs, accs)
        v_new = _each(lambda s_, s: s_[:, :HEAD_DIM] - _dot(s_[:, HEAD_DIM:], s), sol, st)
        outs = _each(lambda o, a, v: o + _dot(a, v), qs, attn, v_new)
        new_s = _each(lambda t, s, k, c, v: jnp.exp(t) * s + _dot_tn(k * jnp.exp(t - c), v), th, st, kn, cc, v_new)
        for (b, h), o, s in zip(grp, outs, new_s):
            o_ref[b, :, lanes(h)] = o
            s_s[b * GDN_HEADS + h] = s


def _gdn_call(cv3, p3, smt3, pc, pr, lc, d, reverse):
    nb, seq, _ = cv3.shape
    nc = seq // CHUNK
    nctx = lc // CHUNK

    def ch(i):
        return _chunk_order(i, nctx, nc, reverse)

    def cspec(col):
        return pl.BlockSpec((nb, CHUNK, GW), lambda i: (0, ch(i), col))

    return pl.pallas_call(
        functools.partial(_gdn_body, reverse=reverse, d=d, nb=nb, group=GDN_GROUP),
        out_shape=jax.ShapeDtypeStruct((nb, seq, GW), f32),
        grid=(nc,),
        in_specs=[cspec(0), cspec(1), cspec(2),
                  pl.BlockSpec((nb, CHUNK, 128), lambda i: (0, ch(i), P_SMALL // 128)),
                  pl.BlockSpec((nb, 32, CHUNK), lambda i: (0, 0, ch(i))),
                  pl.BlockSpec((1, 2, 128), lambda i: (d, 0, 0)),
                  pl.BlockSpec((1, 32, 2), lambda i: (d, 0, 0))],
        out_specs=pl.BlockSpec((nb, CHUNK, GW), lambda i: (0, ch(i), 0)),
        scratch_shapes=[pltpu.VMEM((nb * GDN_HEADS, HEAD_DIM, HEAD_DIM), f32)],
        compiler_params=_cparams(("arbitrary",)),
        name="gdn_rev" if reverse else "gdn_fwd",
    )(cv3, cv3, cv3, p3, smt3, pc, pr)


def _rms_lanes(x, w, eps):
    return x * lax.rsqrt(jnp.mean(x * x, axis=-1, keepdims=True) + eps) * w


def _finish_body(hf, hb, rf, rb, sf, sb, gf, gb, gate, rg, cz, dz, xs, retw, ssdd, ssdw, gdnw, o_ref):
    g = gate[...]
    gelu = 0.5 * g * (1.0 + jnp.tanh(math.sqrt(2.0 / math.pi) * (g + 0.044715 * (g * g * g))))
    o_ref[:, 0:GW] = (gelu * (hf[...] + hb[...])).astype(bf16)

    ro = rf[...] + rb[...]
    rgate = _silu(rg[...])
    go = gf[...] + gb[...]
    ggate = _silu(dz[...])
    for h in range(RET_HEADS):
        lanes = slice(h * HEAD_DIM, (h + 1) * HEAD_DIM)
        o_ref[:, GW + h * HEAD_DIM:GW + (h + 1) * HEAD_DIM] = (
            rgate[:, lanes] * _rms_lanes(ro[:, lanes], retw[:, lanes], 1e-6)).astype(bf16)
        o_ref[:, 3 * GW + h * HEAD_DIM:3 * GW + (h + 1) * HEAD_DIM] = (
            _rms_lanes(go[:, lanes], gdnw[:, lanes], 1e-6) * ggate[:, lanes]).astype(bf16)

    sy = (sf[...] + sb[...] + ssdd[...] * xs[...]) * _silu(cz[...])
    o_ref[:, 2 * GW:3 * GW] = _rms_lanes(sy, ssdw[...], 1e-6).astype(bf16)


def _finish_call(scans, p, cv, retw, ssdd, ssdw, gdnw):
    n = p.shape[0]
    tm = ROW_TILE

    def rowspec(col):
        return pl.BlockSpec((tm, GW), lambda i: (i, col))

    vec = pl.BlockSpec((1, GW), lambda i: (0, 0))
    return pl.pallas_call(
        _finish_body,
        out_shape=jax.ShapeDtypeStruct((n, D_MODEL), bf16),
        grid=(n // tm,),
        in_specs=[rowspec(0)] * 8 + [rowspec(P_GATE // GW), rowspec(P_G // GW), rowspec(P_CZ // GW),
                                     rowspec(P_DZ // GW), rowspec(4), vec, vec, vec, vec],
        out_specs=pl.BlockSpec((tm, D_MODEL), lambda i: (i, 0)),
        compiler_params=_cparams(("arbitrary",)),
        name="mixer_finish",
    )(*scans, p, p, p, p, cv, retw, ssdd, ssdw, gdnw)


def _layer_norm_rows(t, w, b):
    mu = jnp.mean(t, axis=-1, keepdims=True)
    tc = t - mu
    var = jnp.mean(tc * tc, axis=-1, keepdims=True)
    return tc * lax.rsqrt(var + 1e-5) * w + b


HALF = D_MODEL // 2
HI_MASK = 0xFFFF0000


def _pack_bf16_pair(lo, hi):
    lo_bits = lax.bitcast_convert_type(lo.astype(bf16).astype(f32), jnp.uint32)
    hi_bits = lax.bitcast_convert_type(hi.astype(bf16).astype(f32), jnp.uint32)
    return hi_bits | (lo_bits >> 16)


def _unpack_bf16_pair(w):
    lo = lax.bitcast_convert_type(w << 16, f32)
    hi = lax.bitcast_convert_type(w & jnp.uint32(HI_MASK), f32)
    return lo, hi


TOKEN_ROWS = HALF // 128


def _store_token_tiles(ref, packed):
    m = packed.shape[0]
    for c in range(TOKEN_ROWS):
        ref[pl.ds(c, m, stride=TOKEN_ROWS), :] = packed[:, c * 128:(c + 1) * 128]


def _load_token_tiles(ref, m):
    return jnp.concatenate([ref[pl.ds(c, m, stride=TOKEN_ROWS), :] for c in range(TOKEN_ROWS)], axis=1)


def _split_bf16(x):
    hi = x.astype(bf16)
    return hi, (x - hi.astype(f32)).astype(bf16)


def _outproj_body(y_ref, w_ref, x_ref, g1_ref, lw_ref, lb_ref, sc_ref, sh_ref, rwh_ref, rwl_ref,
                  x1_ref, u_ref, lg_ref, *, alpha):
    y = jnp.dot(y_ref[...], w_ref[...], preferred_element_type=f32)
    x1 = _layer_norm_rows(alpha * x_ref[...] + g1_ref[0] * y, lw_ref[...], lb_ref[...])
    x1_ref[...] = x1
    u = x1 * (1.0 + sc_ref[0]) + sh_ref[0]
    _store_token_tiles(u_ref, _pack_bf16_pair(u[:, :HALF], u[:, HALF:]))
    uh, ul = _split_bf16(u)
    rwh = rwh_ref[...]
    lg = (jnp.dot(uh, rwh, preferred_element_type=f32) + jnp.dot(ul, rwh, preferred_element_type=f32)
          + jnp.dot(uh, rwl_ref[...], preferred_element_type=f32))
    lg_ref[...] = lg.T[:N_EXPERTS, :]


def _outproj_call(ycat, w_bf, x, g1, lw, lb, sc2, sh2, rw_hi, rw_lo, alpha):
    n, d = x.shape
    tm = ROW_TILE
    mod = pl.BlockSpec((1, 1, d), lambda i: (i, 0, 0))
    vec = pl.BlockSpec((1, d), lambda i: (0, 0))
    rws = pl.BlockSpec((d, 2 * N_EXPERTS), lambda i: (0, 0))
    return pl.pallas_call(
        functools.partial(_outproj_body, alpha=alpha),
        out_shape=(jax.ShapeDtypeStruct((n, d), f32), jax.ShapeDtypeStruct((n * TOKEN_ROWS, 128), jnp.uint32),
                   jax.ShapeDtypeStruct((N_EXPERTS, n), f32)),
        grid=(n // tm,),
        in_specs=[pl.BlockSpec((tm, d), lambda i: (i, 0)),
                  pl.BlockSpec((d, d), lambda i: (0, 0)),
                  pl.BlockSpec((tm, d), lambda i: (i, 0)),
                  mod, vec, vec, mod, mod, rws, rws],
        out_specs=(pl.BlockSpec((tm, d), lambda i: (i, 0)), pl.BlockSpec((tm * TOKEN_ROWS, 128), lambda i: (i, 0)),
                   pl.BlockSpec((N_EXPERTS, tm), lambda i: (0, i))),
        compiler_params=_cparams(("arbitrary",)),
        name="outproj_ln1",
    )(ycat, w_bf, x, g1, lw, lb, sc2, sh2, rw_hi, rw_lo)


def _route_body(lg_ref, bias_ref, idx_ref, wt_ref):
    tn = lg_ref.shape[1]
    scores = _sigmoid(lg_ref[...])
    sel = scores + bias_ref[...]
    gsz = N_EXPERTS // N_EXPERT_GROUPS
    neg = -jnp.inf
    sel3 = sel.reshape(N_EXPERT_GROUPS, gsz, tn)
    io3 = lax.broadcasted_iota(jnp.int32, sel3.shape, 1)
    m1 = jnp.max(sel3, axis=1)
    first = jnp.min(jnp.where(sel3 == m1[:, None, :], io3, gsz), axis=1)
    m2 = jnp.max(jnp.where(io3 == first[:, None, :], neg, sel3), axis=1)
    gscore = m1 + m2
    iog = lax.broadcasted_iota(jnp.int32, gscore.shape, 0)
    gsel = None
    for _ in range(TOPK_GROUPS):
        gm = jnp.max(gscore, axis=0, keepdims=True)
        gi = jnp.min(jnp.where(gscore == gm, iog, N_EXPERT_GROUPS), axis=0, keepdims=True)
        hit = iog == gi
        gsel = hit if gsel is None else jnp.logical_or(gsel, hit)
        gscore = jnp.where(hit, neg, gscore)
    emask = jnp.broadcast_to(gsel[:, None, :], sel3.shape).reshape(N_EXPERTS, tn)
    cand = jnp.where(emask, sel, neg)
    ioe = lax.broadcasted_iota(jnp.int32, cand.shape, 0)
    idxs = []
    wts = []
    for _ in range(TOP_K):
        cm = jnp.max(cand, axis=0, keepdims=True)
        ci = jnp.min(jnp.where(cand == cm, ioe, N_EXPERTS), axis=0, keepdims=True)
        hit = ioe == ci
        idxs.append(ci)
        wts.append(jnp.sum(jnp.where(hit, scores, 0.0), axis=0, keepdims=True))
        cand = jnp.where(hit, neg, cand)
    wsum = wts[0]
    for w in wts[1:]:
        wsum = wsum + w
    idx_ref[...] = jnp.concatenate(idxs, axis=0)
    wt_ref[...] = jnp.concatenate([w / wsum * ROUTED_SCALE for w in wts], axis=0)


def _route_call(logits_t, bias):
    n = logits_t.shape[1]
    tn = ROW_TILE
    return pl.pallas_call(
        _route_body,
        out_shape=(jax.ShapeDtypeStruct((TOP_K, n), jnp.int32),
                   jax.ShapeDtypeStruct((TOP_K, n), f32)),
        grid=(n // tn,),
        in_specs=[pl.BlockSpec((N_EXPERTS, tn), lambda i: (0, i)),
                  pl.BlockSpec((N_EXPERTS, 1), lambda i: (0, 0))],
        out_specs=(pl.BlockSpec((TOP_K, tn), lambda i: (0, i)),
                   pl.BlockSpec((TOP_K, tn), lambda i: (0, i))),
        compiler_params=_cparams(("arbitrary",)),
        name="router",
    )(logits_t, bias)


EXPERT_TILE = 256


def _num_expert_tiles(n_pairs):
    return (n_pairs + N_EXPERTS * (EXPERT_TILE - 1) + EXPERT_TILE - 1) // EXPERT_TILE


def _moe_plan(idx, wts):
    n = idx.shape[1]
    n_pairs = TOP_K * n
    nt = _num_expert_tiles(n_pairs)
    e_flat = idx.reshape(n_pairs)
    w_flat = wts.reshape(n_pairs)
    order = jnp.argsort(e_flat, stable=True).astype(jnp.int32)
    counts = jnp.sum((e_flat[:, None] == jnp.arange(N_EXPERTS, dtype=jnp.int32)[None, :]).astype(jnp.int32), axis=0)
    tiles_e = (counts + EXPERT_TILE - 1) // EXPERT_TILE
    tend = jnp.cumsum(tiles_e)
    tstart = tend - tiles_e
    cstart = jnp.cumsum(counts) - counts
    n_used = tend[-1:].astype(jnp.int32)
    tile = jnp.arange(nt, dtype=jnp.int32)
    te = jnp.minimum(jnp.searchsorted(tend, tile, side="right"), N_EXPERTS - 1).astype(jnp.int32)
    r = jnp.arange(nt * EXPERT_TILE, dtype=jnp.int32)
    e_r = te[r // EXPERT_TILE]
    j = r - tstart[e_r] * EXPERT_TILE
    valid = jnp.logical_and(j < counts[e_r], r // EXPERT_TILE < n_used[0])
    pair = order[jnp.clip(cstart[e_r] + j, 0, n_pairs - 1)]
    tok = jnp.where(valid, pair % n, 0).reshape(nt, EXPERT_TILE)
    scratch_row = n_pairs + ((r // EXPERT_TILE) % 2) * EXPERT_TILE + r % EXPERT_TILE
    dst = jnp.where(valid, pair, scratch_row).reshape(nt, EXPERT_TILE)
    w_row = jnp.where(valid, w_flat[pair], 0.0).reshape(nt * EXPERT_TILE, 1)
    zeros = jnp.zeros((1, EXPERT_TILE), jnp.int32)
    table = jnp.concatenate([jnp.concatenate([tok, zeros], axis=0),
                             jnp.concatenate([zeros, dst], axis=0)], axis=1)
    return te, n_used, table, w_row


def _experts_body(te_ref, nu_ref, tab_hbm, wrow_ref, w1_ref, w3_ref, w2_ref, u_hbm, comb_hbm,
                  idx_s, xbuf, ybuf, w1_s, w3_s, w2_s, isem, gsem, ssem):
    i = pl.program_id(0)
    nu = nu_ref[0]
    slot = lax.rem(i, 2)
    nslot = 1 - slot
    tm = EXPERT_TILE

    def table_copy(row, s):
        return pltpu.make_async_copy(tab_hbm.at[row], idx_s.at[s], isem.at[s])

    tr = TOKEN_ROWS

    def gather_start(s_idx, s_buf):
        for r in range(tm):
            t = pl.multiple_of(idx_s[s_idx, r] * tr, tr)
            pltpu.make_async_copy(u_hbm.at[pl.ds(t, tr)], xbuf.at[s_buf, pl.ds(r * tr, tr)],
                                  gsem.at[s_buf]).start(priority=r % 2)

    def gather_wait(s_buf):
        pltpu.make_async_copy(u_hbm.at[pl.ds(0, tm * tr)], xbuf.at[s_buf], gsem.at[s_buf]).wait()

    def scatter_start(s_idx, s_buf):
        for r in range(tm):
            t = pl.multiple_of(idx_s[s_idx, tm + r] * tr, tr)
            pltpu.make_async_copy(ybuf.at[s_buf, pl.ds(r * tr, tr)], comb_hbm.at[pl.ds(t, tr)],
                                  ssem.at[s_buf]).start(priority=r % 2)

    def scatter_wait(s_buf):
        pltpu.make_async_copy(ybuf.at[s_buf], comb_hbm.at[pl.ds(0, tm * tr)], ssem.at[s_buf]).wait()

    @pl.when(i == 0)
    def _():
        first = table_copy(0, 0)
        first.start()
        first.wait()
        gather_start(0, 0)
        table_copy(1, 1).start()
        ybuf[...] = jnp.zeros_like(ybuf)
        n_pairs = comb_hbm.shape[0] // tr - 2 * tm
        for s in range(2):
            fill = pltpu.make_async_copy(ybuf.at[s], comb_hbm.at[pl.ds((n_pairs + s * tm) * tr, tm * tr)],
                                         ssem.at[s])
            fill.start()
            fill.wait()

    @pl.when(i < nu)
    def _():
        table_copy(i + 1, nslot).wait()

        @pl.when(i + 1 < nu)
        def _():
            gather_start(nslot, nslot)

        @pl.when(i + 2 <= nu)
        def _():
            table_copy(i + 2, slot).start()

        prev = te_ref[jnp.maximum(i - 1, 0)]

        @pl.when(jnp.logical_or(i == 0, te_ref[i] != prev))
        def _():
            w1_s[...] = w1_ref[0, 0].astype(bf16)
            w3_s[...] = w3_ref[0, 0].astype(bf16)
            w2_s[...] = w2_ref[0, 0].astype(bf16)

        gather_wait(slot)
        lo, hi = _unpack_bf16_pair(_load_token_tiles(xbuf.at[slot], tm))
        lo = lo.astype(bf16)
        hi = hi.astype(bf16)
        h1 = (jnp.dot(lo, w1_s[:HALF, :], preferred_element_type=f32)
              + jnp.dot(hi, w1_s[HALF:, :], preferred_element_type=f32))
        h3 = (jnp.dot(lo, w3_s[:HALF, :], preferred_element_type=f32)
              + jnp.dot(hi, w3_s[HALF:, :], preferred_element_type=f32))
        h = _silu(h1) * h3 * wrow_ref[...]
        y = jnp.dot(h.astype(bf16), w2_s[...], preferred_element_type=f32)

        @pl.when(i >= 2)
        def _():
            scatter_wait(slot)

        _store_token_tiles(ybuf.at[slot], _pack_bf16_pair(y[:, :HALF], y[:, HALF:]))
        scatter_start(nslot, slot)

        @pl.when(i == nu - 1)
        def _():
            scatter_wait(slot)

            @pl.when(i >= 1)
            def _():
                scatter_wait(nslot)


def _experts_call(te, n_used, table, w_row, w1, w3, w2, u_pk, layer):
    n = u_pk.shape[0] // TOKEN_ROWS
    n_pairs = TOP_K * n
    nt = te.shape[0]
    _, ne, d, fe = w1.shape
    tm = EXPERT_TILE
    grid_spec = pltpu.PrefetchScalarGridSpec(
        num_scalar_prefetch=2,
        grid=(nt,),
        in_specs=[pl.BlockSpec(memory_space=pl.ANY),
                  pl.BlockSpec((tm, 1), lambda i, te_r, nu_r: (i, 0)),
                  pl.BlockSpec((1, 1, d, fe), lambda i, te_r, nu_r: (layer, te_r[i], 0, 0)),
                  pl.BlockSpec((1, 1, d, fe), lambda i, te_r, nu_r: (layer, te_r[i], 0, 0)),
                  pl.BlockSpec((1, 1, fe, d), lambda i, te_r, nu_r: (layer, te_r[i], 0, 0)),
                  pl.BlockSpec(memory_space=pl.ANY)],
        out_specs=pl.BlockSpec(memory_space=pl.ANY),
        scratch_shapes=[pltpu.SMEM((2, 2 * tm), jnp.int32),
                        pltpu.VMEM((2, tm * TOKEN_ROWS, 128), jnp.uint32),
                        pltpu.VMEM((2, tm * TOKEN_ROWS, 128), jnp.uint32),
                        pltpu.VMEM((d, fe), bf16), pltpu.VMEM((d, fe), bf16), pltpu.VMEM((fe, d), bf16),
                        pltpu.SemaphoreType.DMA((2,)), pltpu.SemaphoreType.DMA((2,)),
                        pltpu.SemaphoreType.DMA((2,))])
    return pl.pallas_call(
        _experts_body,
        out_shape=jax.ShapeDtypeStruct(((n_pairs + 2 * tm) * TOKEN_ROWS, 128), jnp.uint32),
        grid_spec=grid_spec,
        compiler_params=_cparams(("arbitrary",)),
        name="experts",
    )(te, n_used, table, w_row, w1, w3, w2, u_pk)


def _combine_body(*refs, alpha):
    comb = refs[:TOP_K]
    u_ref, x_ref, g2_ref, lw_ref, lb_ref, s1_ref, s3_ref, s2_ref, o_ref = refs[TOP_K:]
    tm = x_ref.shape[0]
    lo, hi = _unpack_bf16_pair(_load_token_tiles(comb[0], tm))
    for c in comb[1:]:
        lo_k, hi_k = _unpack_bf16_pair(_load_token_tiles(c, tm))
        lo = lo + lo_k
        hi = hi + hi_k
    ulo, uhi = _unpack_bf16_pair(_load_token_tiles(u_ref, tm))
    ulo = ulo.astype(bf16)
    uhi = uhi.astype(bf16)
    a1 = (jnp.dot(ulo, s1_ref[:HALF, :], preferred_element_type=f32)
          + jnp.dot(uhi, s1_ref[HALF:, :], preferred_element_type=f32))
    a3 = (jnp.dot(ulo, s3_ref[:HALF, :], preferred_element_type=f32)
          + jnp.dot(uhi, s3_ref[HALF:, :], preferred_element_type=f32))
    shared = jnp.dot((_silu(a1) * a3).astype(bf16), s2_ref[...], preferred_element_type=f32)
    f = jnp.concatenate([lo, hi], axis=1) + shared
    o_ref[...] = _layer_norm_rows(alpha * x_ref[...] + g2_ref[0] * f, lw_ref[...], lb_ref[...])


def _combine_call(comb, u_pk, x1, g2, lw, lb, s1, s3, s2, alpha):
    n, d = x1.shape
    tm = ROW_TILE
    nblk = n // tm
    fe = s1.shape[1]
    row = pl.BlockSpec((tm, d), lambda i: (i, 0))
    vec = pl.BlockSpec((1, d), lambda i: (0, 0))
    tok = pl.BlockSpec((tm * TOKEN_ROWS, 128), lambda i: (i, 0))
    comb_specs = [pl.BlockSpec((tm * TOKEN_ROWS, 128), lambda i, k=k: (k * nblk + i, 0)) for k in range(TOP_K)]
    return pl.pallas_call(
        functools.partial(_combine_body, alpha=alpha),
        out_shape=jax.ShapeDtypeStruct((n, d), f32),
        grid=(nblk,),
        in_specs=comb_specs + [tok, row,
                               pl.BlockSpec((1, 1, d), lambda i: (i, 0, 0)), vec, vec,
                               pl.BlockSpec((d, fe), lambda i: (0, 0)), pl.BlockSpec((d, fe), lambda i: (0, 0)),
                               pl.BlockSpec((fe, d), lambda i: (0, 0))],
        out_specs=row,
        compiler_params=_cparams(("arbitrary",)),
        name="combine_ln2",
    )(*([comb] * TOP_K), u_pk, x1, g2, lw, lb, s1, s3, s2)


def _block_diag(w):
    g, n, _ = w.shape
    eye = jnp.eye(g, dtype=w.dtype)
    return (eye[:, None, :, None] * w[:, :, None, :]).reshape(g * n, g * n)


def _rope_tables(seq, lc):
    half = HEAD_DIM // 2
    t = jnp.arange(seq - lc, dtype=f32)
    inv = ROPE_BASE ** (-jnp.arange(0, half, 2, dtype=f32) / half)
    ar = jnp.floor(t / GRID_W)[:, None] * inv[None, :]
    ac = (t - jnp.floor(t / GRID_W) * GRID_W)[:, None] * inv[None, :]
    cos = jnp.concatenate([jnp.cos(ar), jnp.cos(ar), jnp.cos(ac), jnp.cos(ac)], axis=1)
    sin = jnp.concatenate([-jnp.sin(ar), jnp.sin(ar), -jnp.sin(ac), jnp.sin(ac)], axis=1)
    cos = jnp.concatenate([jnp.ones((lc, HEAD_DIM), f32), cos], axis=0)
    sin = jnp.concatenate([jnp.zeros((lc, HEAD_DIM), f32), sin], axis=0)
    return cos, sin


def _pad_lanes(v, offset, width=128):
    return jnp.zeros((width,), f32).at[offset:offset + v.shape[0]].set(v)


def kernel(x, c, ctx, c_ctx, w_ada, b_ada, w_in, lru_conv_w, lru_conv_b, lru_gate_a_w, lru_gate_a_b,
           lru_gate_x_w, lru_gate_x_b, lru_lambda, ret_log_decay, ret_norm_w, ssd_conv_w, ssd_conv_b,
           ssd_dt_bias, ssd_a_log, ssd_d, ssd_norm_w, gdn_conv_w, gdn_dt_bias, gdn_a_log, gdn_norm_w,
           w_out, ln1_w, ln1_b, router_w, router_bias, exp_w1, exp_w3, exp_w2, sh_w1, sh_w3, sh_w2,
           ln2_w, ln2_b):
    nb, lat, d = x.shape
    lc = ctx.shape[1]
    depth = w_ada.shape[0]
    seq = lc + lat
    n = nb * seq
    ntile = n // ROW_TILE
    tiles_per_seq = seq // ROW_TILE
    alpha = (2 * depth) ** 0.25
    assert d == D_MODEL and lc == ROW_TILE and lat % ROW_TILE == 0 and nb <= 7

    xall = jnp.concatenate([ctx, x], axis=1).reshape(n, d)

    svec = jnp.zeros((8, d), f32).at[0].set(c_ctx).at[1:1 + nb].set(c)
    modtab = _ada_call(svec, w_ada, b_ada).reshape(depth, 8, 6, d)
    tile = jnp.arange(ntile)
    tile_row = jnp.where(tile % tiles_per_seq == 0, 0, 1 + tile // tiles_per_seq)
    cos, sin = _rope_tables(seq, lc)

    for l in range(depth):
        mods = [modtab[l, :, k, :][tile_row][:, None, :] for k in range(6)]
        sh1, sc1, g1, sh2, sc2, g2 = mods

        wl = w_in[l]
        w_re = jnp.concatenate([wl[:, :3584], wl[:, 5896:6408], wl[:, 4360:5896], wl[:, 3584:4352],
                                wl[:, 4352:4360], wl[:, 6408:6424],
                                jnp.zeros((d, IN_COLS_PAD - 6424), f32)], axis=1).astype(bf16)
        p = _inproj_call(xall, sc1, sh1, w_re)
        smt = p[:, P_SMALL:P_SMALL + 32].T

        wcat = jnp.concatenate([gdn_conv_w[l], lru_conv_w[l], ssd_conv_w[l]], axis=1)
        bcat = jnp.concatenate([jnp.zeros((3 * GW,), f32), lru_conv_b[l], ssd_conv_b[l]])[None, :]
        cv = _conv_call(p, wcat, bcat, nb, seq, lc)

        wg = jnp.stack([jnp.concatenate([_block_diag(lru_gate_a_w[l, dd]), _block_diag(lru_gate_x_w[l, dd])],
                                        axis=1) for dd in range(2)]).astype(bf16)
        bg = jnp.concatenate([lru_gate_a_b[l], lru_gate_x_b[l]], axis=1)[:, None, :]
        sp = jax.nn.softplus(-lru_lambda[l])[:, None, :]
        hs = [_lru_call(cv, wg, bg, sp, nb, seq, lc, dd, bool(dd)) for dd in range(2)]

        lg = -jnp.exp(ret_log_decay[l].astype(f32))
        rs = [_ret_call(p, lg[dd], cos, sin, nb, seq, lc, bool(dd)) for dd in range(2)]

        a_ssd = -jnp.exp(ssd_a_log[l])
        pc = jnp.stack([jnp.stack([_pad_lanes(ssd_dt_bias[l, dd], 0), _pad_lanes(a_ssd[dd], 0)])
                        for dd in range(2)])
        pr = jnp.stack([jnp.stack([ssd_dt_bias[l, dd], a_ssd[dd]], axis=1) for dd in range(2)])
        ss = [_ssd_call(cv, p, smt, pc, pr, nb, seq, lc, dd, bool(dd)) for dd in range(2)]

        a_gdn = -jnp.exp(gdn_a_log[l])
        pcg = jnp.stack([jnp.stack([_pad_lanes(gdn_dt_bias[l, dd], 8 + 4 * dd),
                                    _pad_lanes(a_gdn[dd], 8 + 4 * dd)]) for dd in range(2)])
        prg = jnp.stack([jnp.stack([_pad_lanes(gdn_dt_bias[l, dd], 8 + 4 * dd, 32),
                                    _pad_lanes(a_gdn[dd], 8 + 4 * dd, 32)], axis=1) for dd in range(2)])
        p3 = p.reshape(nb, seq, IN_COLS_PAD)
        cv3 = cv.reshape(nb, seq, CV_COLS)
        smt3 = jnp.transpose(p3[:, :, P_SMALL:P_SMALL + 32], (0, 2, 1))
        gs = [_gdn_call(cv3, p3, smt3, pcg, prg, lc, dd, bool(dd)).reshape(n, GW) for dd in range(2)]

        ycat = _finish_call(hs + rs + ss + gs, p, cv, ret_norm_w[l][None, :],
                            jnp.repeat(ssd_d[l], SSD_HEAD_DIM)[None, :], ssd_norm_w[l][None, :],
                            jnp.tile(gdn_norm_w[l], GDN_HEADS)[None, :])

        rw = jnp.concatenate([router_w[l], jnp.zeros((d, N_EXPERTS), f32)], axis=1)
        rw_hi = rw.astype(bf16)
        rw_lo = (rw - rw_hi.astype(f32)).astype(bf16)
        x1, u_pk, logits_t = _outproj_call(ycat, w_out[l].astype(bf16), xall, g1, ln1_w[l][None, :],
                                           ln1_b[l][None, :], sc2, sh2, rw_hi, rw_lo, alpha)
        idx, wts = _route_call(logits_t, router_bias[l][:, None])
        te, n_used, table, w_row = _moe_plan(idx, wts)
        comb = _experts_call(te, n_used, table, w_row, exp_w1, exp_w3, exp_w2, u_pk, l)
        xall = _combine_call(comb, u_pk, x1, g2, ln2_w[l][None, :], ln2_b[l][None, :],
                             sh_w1[l].astype(bf16), sh_w3[l].astype(bf16), sh_w2[l].astype(bf16), alpha)

    return xall.reshape(nb, seq, d)[:, lc:, :]
```

```python
import functools
import math

import jax
import jax.numpy as jnp
from jax import lax
from jax.experimental import pallas as pl
from jax.experimental.pallas import tpu as pltpu

f32 = jnp.float32
bf16 = jnp.bfloat16
HIGHEST = lax.Precision.HIGHEST

D_MODEL = 2048
GW = 512
GRID_W = 64
CONV_K = 4
LRU_C = 8.0
RET_HEADS = 4
HEAD_DIM = 128
ROPE_BASE = 10000.0
SSD_HEADS = 8
SSD_HEAD_DIM = 64
SSD_STATE = 64
GDN_HEADS = 4
N_EXPERTS = 64
TOP_K = 8
N_EXPERT_GROUPS = 8
TOPK_GROUPS = 4
D_EXPERT = 256
ROUTED_SCALE = 2.5

ROW_TILE = 256
CHUNK = 128
IN_COLS_PAD = 6528
CV_COLS = 2816
VMEM_LIMIT = 56 * 1024 * 1024

P_XB, P_GATE, P_Q, P_K, P_V, P_G, P_CZ, P_DZ, P_DQKV, P_XBC, P_SMALL = (
    0, 512, 1024, 1536, 2048, 2560, 3072, 3584, 4096, 5632, 6400)


def _cparams(sem):
    return pltpu.CompilerParams(dimension_semantics=sem, vmem_limit_bytes=VMEM_LIMIT)


def _sigmoid(x):
    return 1.0 / (1.0 + jnp.exp(-x))


def _silu(x):
    return x * _sigmoid(x)


def _softplus(x):
    return jnp.maximum(x, 0.0) + jnp.log(1.0 + jnp.exp(-jnp.abs(x)))


def _dot(a, b):
    return jnp.dot(a.astype(bf16), b.astype(bf16), preferred_element_type=f32)


def _dot_nt(a, b):
    return lax.dot_general(a.astype(bf16), b.astype(bf16), (((1,), (1,)), ((), ())),
                           preferred_element_type=f32)


def _dot_tn(a, b):
    return lax.dot_general(a.astype(bf16), b.astype(bf16), (((0,), (0,)), ((), ())),
                           preferred_element_type=f32)


def _chunk_order(i, n_ctx, n_all, reverse):
    if not reverse:
        return i
    return jnp.where(i < n_ctx, n_ctx - 1 - i, n_all + n_ctx - 1 - i)


def _ada_body(s_ref, w_ref, b_ref, o_ref):
    x = s_ref[...]
    o_ref[0] = jnp.dot(_silu(x), w_ref[0], precision=HIGHEST, preferred_element_type=f32) + b_ref[0]


def _ada_call(svec, w_ada, b_ada):
    nl, d, d6 = w_ada.shape
    tn = 1024
    return pl.pallas_call(
        _ada_body,
        out_shape=jax.ShapeDtypeStruct((nl, 8, d6), f32),
        grid=(nl, d6 // tn),
        in_specs=[pl.BlockSpec((8, d), lambda l, j: (0, 0)),
                  pl.BlockSpec((1, d, tn), lambda l, j: (l, 0, j)),
                  pl.BlockSpec((1, 1, tn), lambda l, j: (l, 0, j))],
        out_specs=pl.BlockSpec((1, 8, tn), lambda l, j: (l, 0, j)),
        compiler_params=_cparams(("arbitrary", "arbitrary")),
        name="ada",
    )(svec, w_ada, b_ada.reshape(nl, 1, d6))


def _inproj_body(x_ref, sc_ref, sh_ref, w_ref, o_ref, *, sub):
    for s in range(sub):
        rows = slice(s * ROW_TILE, (s + 1) * ROW_TILE)
        xm = x_ref[rows, :] * (1.0 + sc_ref[s]) + sh_ref[s]
        o_ref[rows, :] = _dot(xm, w_ref[...])


def _inproj_call(x, sc, sh, w_bf):
    n, d = x.shape
    sub = 2
    tm = sub * ROW_TILE
    tn = IN_COLS_PAD // 3
    return pl.pallas_call(
        functools.partial(_inproj_body, sub=sub),
        out_shape=jax.ShapeDtypeStruct((n, IN_COLS_PAD), f32),
        grid=(3, n // tm),
        in_specs=[pl.BlockSpec((tm, d), lambda j, i: (i, 0)),
                  pl.BlockSpec((sub, 1, d), lambda j, i: (i, 0, 0)),
                  pl.BlockSpec((sub, 1, d), lambda j, i: (i, 0, 0)),
                  pl.BlockSpec((d, tn), lambda j, i: (0, j))],
        out_specs=pl.BlockSpec((tm, tn), lambda j, i: (i, j)),
        compiler_params=_cparams(("arbitrary", "arbitrary")),
        name="inproj",
    )(x, sc, sh, w_bf)


def _conv_body(x_ref, w_ref, b_ref, o_ref, *, seq, lc):
    j = pl.program_id(1)
    w = w_ref[...]
    bias = b_ref[...]
    use_act = jnp.logical_or(j < 6, j >= 8)
    row = lax.broadcasted_iota(jnp.int32, (CHUNK, 1), 0)

    def chunk(c, carry):
        r0 = pl.multiple_of(c * CHUNK, CHUNK)
        cur = x_ref[pl.ds(r0, CHUNK), :]
        prev = x_ref[pl.ds(pl.multiple_of(jnp.maximum(r0 - 8, 0), 8), 8), :]
        nxt = x_ref[pl.ds(pl.multiple_of(jnp.minimum(r0 + CHUNK, seq - 8), 8), 8), :]
        ext = jnp.concatenate([prev, cur, nxt], axis=0)
        t = r0 + row
        s0 = jnp.where(r0 < lc, 0, lc)
        s1 = jnp.where(r0 < lc, lc, seq)
        acc = jnp.zeros_like(cur) + bias
        for k in range(CONV_K):
            o = k - CONV_K // 2
            seg = ext[8 + o:8 + o + CHUNK]
            valid = jnp.logical_and(t + o >= s0, t + o < s1)
            acc = acc + jnp.where(valid, seg, 0.0) * w[k:k + 1, :]
        o_ref[pl.ds(r0, CHUNK), :] = jnp.where(use_act, _silu(acc), acc)
        return carry

    lax.fori_loop(0, seq // CHUNK, chunk, 0)


def _conv_call(p, wcat, bcat, nb, seq, lc):
    n = p.shape[0]
    cb = 256

    def in_map(b, j):
        return (b, jnp.where(j < 6, P_DQKV // cb + j, jnp.where(j < 8, j - 6, P_XBC // cb - 8 + j)))

    return pl.pallas_call(
        functools.partial(_conv_body, seq=seq, lc=lc),
        out_shape=jax.ShapeDtypeStruct((n, CV_COLS), f32),
        grid=(nb, CV_COLS // cb),
        in_specs=[pl.BlockSpec((seq, cb), in_map),
                  pl.BlockSpec((CONV_K, cb), lambda b, j: (0, j)),
                  pl.BlockSpec((1, cb), lambda b, j: (0, j))],
        out_specs=pl.BlockSpec((seq, cb), lambda b, j: (b, j)),
        compiler_params=_cparams(("arbitrary", "arbitrary")),
        name="dwconv",
    )(p, wcat, bcat)


def _lru_body(xc_ref, wg_ref, bg_ref, sp_ref, o_ref, a_s, b_s, h_s, *, reverse):
    i = pl.program_id(1)

    @pl.when(i == 0)
    def _():
        h_s[...] = jnp.zeros_like(h_s)

    xc = xc_ref[...]
    gates = _dot(xc, wg_ref[0]) + bg_ref[0]
    r = _sigmoid(gates[:, :GW])
    ig = _sigmoid(gates[:, GW:])
    a = jnp.exp(-LRU_C * r * sp_ref[0])
    a_s[...] = a
    b_s[...] = jnp.sqrt(1.0 - a * a) * (ig * xc)

    def step(jj, h):
        t = (ROW_TILE - 1 - jj) if reverse else jj
        h = a_s[pl.ds(t, 1), :] * h + b_s[pl.ds(t, 1), :]
        o_ref[pl.ds(t, 1), :] = h
        return h

    h_s[...] = lax.fori_loop(0, ROW_TILE, step, h_s[...], unroll=8)


def _lru_call(cv, wg, bg, sp, nb, seq, lc, d, reverse):
    n = cv.shape[0]
    nt = seq // ROW_TILE
    nctx = lc // ROW_TILE

    def rows(b, i):
        return b * nt + _chunk_order(i, nctx, nt, reverse)

    return pl.pallas_call(
        functools.partial(_lru_body, reverse=reverse),
        out_shape=jax.ShapeDtypeStruct((n, GW), f32),
        grid=(nb, nt),
        in_specs=[pl.BlockSpec((ROW_TILE, GW), lambda b, i: (rows(b, i), 3)),
                  pl.BlockSpec((1, GW, 2 * GW), lambda b, i: (d, 0, 0)),
                  pl.BlockSpec((1, 1, 2 * GW), lambda b, i: (d, 0, 0)),
                  pl.BlockSpec((1, 1, GW), lambda b, i: (d, 0, 0))],
        out_specs=pl.BlockSpec((ROW_TILE, GW), lambda b, i: (rows(b, i), 0)),
        scratch_shapes=[pltpu.VMEM((ROW_TILE, GW), f32), pltpu.VMEM((ROW_TILE, GW), f32),
                        pltpu.VMEM((1, GW), f32)],
        compiler_params=_cparams(("arbitrary", "arbitrary")),
        name="lru_rev" if reverse else "lru_fwd",
    )(cv, wg, bg, sp)


def _tri_masks(reverse):
    ii = lax.broadcasted_iota(jnp.int32, (CHUNK, CHUNK), 0)
    jj = lax.broadcasted_iota(jnp.int32, (CHUNK, CHUNK), 1)
    if reverse:
        return jj >= ii, jj > ii, ii, jj
    return ii >= jj, ii > jj, ii, jj


def _rope(x, cos, sin):
    lane = lax.broadcasted_iota(jnp.int32, x.shape, 1)
    swapped = jnp.where(lane % 64 < 32, pltpu.roll(x, 96, 1), pltpu.roll(x, 32, 1))
    return x * cos + swapped * sin


def _ret_body(lg_ref, q_ref, k_ref, v_ref, cos_ref, sin_ref, o_ref, s_s, *, reverse):
    i = pl.program_id(1)

    @pl.when(i == 0)
    def _():
        s_s[...] = jnp.zeros_like(s_s)

    mask, _, ii, jj = _tri_masks(reverse)
    diff = ((jj - ii) if reverse else (ii - jj)).astype(f32)
    col = lax.broadcasted_iota(jnp.int32, (CHUNK, 1), 0).astype(f32)
    cos = cos_ref[...]
    sin = sin_ref[...]
    for h in range(RET_HEADS):
        lg = lg_ref[h]
        lanes = slice(h * HEAD_DIM, (h + 1) * HEAD_DIM)
        qh = _rope(q_ref[:, lanes], cos, sin) * (HEAD_DIM ** -0.5)
        kh = _rope(k_ref[:, lanes], cos, sin)
        vh = v_ref[:, lanes]
        dec = jnp.where(mask, jnp.exp(lg * diff), 0.0)
        if reverse:
            eq = jnp.exp(lg * (CHUNK - col))
            ek = jnp.exp(lg * col)
        else:
            eq = jnp.exp(lg * (col + 1.0))
            ek = jnp.exp(lg * (CHUNK - 1.0 - col))
        st = s_s[h]
        y = _dot(_dot_nt(qh, kh) * dec, vh) + _dot(qh * eq, st)
        s_s[h] = jnp.exp(lg * CHUNK) * st + _dot_tn(kh * ek, vh)
        o_ref[:, lanes] = y


def _ret_call(p, lg, cos, sin, nb, seq, lc, reverse):
    n = p.shape[0]
    nc = seq // CHUNK
    nctx = lc // CHUNK

    def ch(i):
        return _chunk_order(i, nctx, nc, reverse)

    def pspec(col):
        return pl.BlockSpec((CHUNK, GW), lambda b, i: (b * nc + ch(i), col // GW))

    return pl.pallas_call(
        functools.partial(_ret_body, reverse=reverse),
        out_shape=jax.ShapeDtypeStruct((n, GW), f32),
        grid=(nb, nc),
        in_specs=[pl.BlockSpec(memory_space=pltpu.SMEM),
                  pspec(P_Q), pspec(P_K), pspec(P_V),
                  pl.BlockSpec((CHUNK, HEAD_DIM), lambda b, i: (ch(i), 0)),
                  pl.BlockSpec((CHUNK, HEAD_DIM), lambda b, i: (ch(i), 0))],
        out_specs=pl.BlockSpec((CHUNK, GW), lambda b, i: (b * nc + ch(i), 0)),
        scratch_shapes=[pltpu.VMEM((RET_HEADS, HEAD_DIM, HEAD_DIM), f32)],
        compiler_params=_cparams(("arbitrary", "arbitrary")),
        name="ret_rev" if reverse else "ret_fwd",
    )(lg, p, p, p, cos, sin)


def _cumsums(la_col, la_row, reverse):
    mask, _, ii, jj = _tri_masks(reverse)
    m_col = jnp.where(mask, 1.0, 0.0)
    m_row = jnp.where(jnp.logical_not(mask) | (ii == jj), 1.0, 0.0)
    cum_col = jnp.dot(m_col, la_col, precision=HIGHEST, preferred_element_type=f32)
    cum_row = jnp.dot(la_row, m_row, precision=HIGHEST, preferred_element_type=f32)
    return cum_col, cum_row, mask


def _ssd_body(xs_ref, bc_ref, sm_ref, smt_ref, pc_ref, pr_ref, o_ref, s_s, *, reverse):
    i = pl.program_id(1)

    @pl.when(i == 0)
    def _():
        s_s[...] = jnp.zeros_like(s_s)

    dt_col = _softplus(sm_ref[...] + pc_ref[0, 0:1, :])
    la_col = pc_ref[0, 1:2, :] * dt_col
    dt_row = _softplus(smt_ref[...] + pr_ref[0, :, 0:1])
    la_row = pr_ref[0, :, 1:2] * dt_row
    cum_col, cum_row, mask = _cumsums(la_col, la_row, reverse)
    tot = cum_col[0:1, :] if reverse else cum_col[CHUNK - 1:CHUNK, :]
    bm = bc_ref[:, :128]
    cm = bc_ref[:, 128:]
    lane = lax.broadcasted_iota(jnp.int32, (1, 128), 1)
    for g in range(2):
        gmask = (lane // SSD_STATE) == g
        cmg = jnp.where(gmask, cm, 0.0)
        bmg = jnp.where(gmask, bm, 0.0)
        s = _dot_nt(cmg, bmg)
        for pp in range(2 * g, 2 * g + 2):
            lanes = slice(pp * 128, (pp + 1) * 128)
            xpair = xs_ref[:, lanes]
            st = s_s[pp]
            y = jnp.zeros((CHUNK, 128), f32)
            new_s = jnp.zeros((128, 128), f32)
            for hh in range(2):
                h = 2 * pp + hh
                hmask = (lane // SSD_HEAD_DIM) == hh
                cc = cum_col[:, h:h + 1]
                cr = cum_row[h:h + 1, :]
                dec = jnp.where(mask, jnp.exp(cc - cr), 0.0)
                vh = jnp.where(hmask, xpair * dt_col[:, h:h + 1], 0.0)
                sth = jnp.where(hmask, st, 0.0)
                th = tot[:, h:h + 1]
                y = y + _dot(s * dec, vh) + _dot(cmg * jnp.exp(cc), sth)
                new_s = new_s + jnp.exp(th) * sth + _dot_tn(bmg * jnp.exp(th - cc), vh)
            s_s[pp] = new_s
            o_ref[:, lanes] = y


def _ssd_call(cv, p, smt, pc, pr, nb, seq, lc, d, reverse):
    n = cv.shape[0]
    nc = seq // CHUNK
    nctx = lc // CHUNK

    def row(b, i):
        return b * nc + _chunk_order(i, nctx, nc, reverse)

    return pl.pallas_call(
        functools.partial(_ssd_body, reverse=reverse),
        out_shape=jax.ShapeDtypeStruct((n, GW), f32),
        grid=(nb, nc),
        in_specs=[pl.BlockSpec((CHUNK, GW), lambda b, i: (row(b, i), 4)),
                  pl.BlockSpec((CHUNK, 256), lambda b, i: (row(b, i), 10)),
                  pl.BlockSpec((CHUNK, 128), lambda b, i: (row(b, i), P_SMALL // 128)),
                  pl.BlockSpec((8, CHUNK), lambda b, i: (0, row(b, i))),
                  pl.BlockSpec((1, 2, 128), lambda b, i: (d, 0, 0)),
                  pl.BlockSpec((1, 8, 2), lambda b, i: (d, 0, 0))],
        out_specs=pl.BlockSpec((CHUNK, GW), lambda b, i: (row(b, i), 0)),
        scratch_shapes=[pltpu.VMEM((4, 128, 128), f32)],
        compiler_params=_cparams(("arbitrary", "arbitrary")),
        name="ssd_rev" if reverse else "ssd_fwd",
    )(cv, cv, p, smt, pc, pr)


TRI_BASE = 8
GDN_GROUP = 16


def _each(fn, *lists):
    return [fn(*args) for args in zip(*lists)]


def _unit_tri_inverse_minus_eye(ms, ii, jj):
    base = (ii // TRI_BASE) == (jj // TRI_BASE)
    pws = _each(lambda m: jnp.where(base, -m, 0.0), ms)
    accs = pws
    size = 2
    while size < TRI_BASE:
        pws = _each(lambda p: _dot(p, p), pws)
        accs = _each(lambda a, p: a + p + _dot(a, p), accs, pws)
        size *= 2
    half = TRI_BASE
    while half < CHUNK:
        off = jnp.logical_and((ii // (2 * half)) == (jj // (2 * half)), (ii // half) != (jj // half))
        mos = _each(lambda m: jnp.where(off, m, 0.0), ms)
        xs = _each(lambda a, mo: mo + _dot(a, mo), accs, mos)
        accs = _each(lambda a, x: a - x - _dot(x, a), accs, xs)
        half *= 2
    return accs


def _gdn_body(q_ref, k_ref, v_ref, sm_ref, smt_ref, pc_ref, pr_ref, o_ref, s_s, *, reverse, d, nb, group):
    i = pl.program_id(0)

    @pl.when(i == 0)
    def _():
        s_s[...] = jnp.zeros_like(s_s)

    mask, strict, blk_i, blk_j = _tri_masks(reverse)
    per_b = []
    for b in range(nb):
        sm = sm_ref[b]
        la_col = pc_ref[0, 1:2, :] * _softplus(sm + pc_ref[0, 0:1, :])
        la_row = pr_ref[0, :, 1:2] * _softplus(smt_ref[b] + pr_ref[0, :, 0:1])
        cum_col, cum_row, _ = _cumsums(la_col, la_row, reverse)
        tot = cum_col[0:1, :] if reverse else cum_col[CHUNK - 1:CHUNK, :]
        per_b.append((cum_col, cum_row, tot, _sigmoid(sm)))

    chains = [(b, h) for b in range(nb) for h in range(GDN_HEADS)]
    for g0 in range(0, len(chains), group):
        grp = chains[g0:g0 + group]

        def lanes(h):
            return slice(h * HEAD_DIM, (h + 1) * HEAD_DIM)

        def l2n(x):
            return x * lax.rsqrt(jnp.sum(x * x, axis=-1, keepdims=True) + 1e-6)

        ca = [8 + d * GDN_HEADS + h for _, h in grp]
        cb = [16 + d * GDN_HEADS + h for _, h in grp]
        qn = [l2n(q_ref[b, :, lanes(h)]) * (HEAD_DIM ** -0.5) for b, h in grp]
        kn = [l2n(k_ref[b, :, lanes(h)]) for b, h in grp]
        vh = [v_ref[b, :, lanes(h)] for b, h in grp]
        cc = [per_b[b][0][:, c:c + 1] for (b, _), c in zip(grp, ca)]
        cr = [per_b[b][1][c:c + 1, :] for (b, _), c in zip(grp, ca)]
        th = [per_b[b][2][:, c:c + 1] for (b, _), c in zip(grp, ca)]
        beta = [per_b[b][3][:, c:c + 1] for (b, _), c in zip(grp, cb)]
        st = [s_s[b * GDN_HEADS + h] for b, h in grp]
        dec = _each(lambda c, r: jnp.where(mask, jnp.exp(c - r), 0.0), cc, cr)
        kb = _each(lambda k, bt: k * bt, kn, beta)
        ms = _each(lambda a, k, dc: jnp.where(strict, _dot_nt(a, k) * dc, 0.0), kb, kn, dec)
        attn = _each(lambda q, k, dc: _dot_nt(q, k) * dc, qn, kn, dec)
        qs = _each(lambda q, c, s: _dot(q * jnp.exp(c), s), qn, cc, st)
        accs = _unit_tri_inverse_minus_eye(ms, blk_i, blk_j)
        rhs = _each(lambda v, bt, k, c: jnp.concatenate([v * bt, k * jnp.exp(c)], axis=1), vh, beta, kb, cc)
        sol = _each(lambda r, a: r + _dot(a, r), rhs, accs)
        v_new = _each(lambda s_, s: s_[:, :HEAD_DIM] - _dot(s_[:, HEAD_DIM:], s), sol, st)
        outs = _each(lambda o, a, v: o + _dot(a, v), qs, attn, v_new)
        new_s = _each(lambda t, s, k, c, v: jnp.exp(t) * s + _dot_tn(k * jnp.exp(t - c), v), th, st, kn, cc, v_new)
        for (b, h), o, s in zip(grp, outs, new_s):
            o_ref[b, :, lanes(h)] = o
            s_s[b * GDN_HEADS + h] = s


def _gdn_call(cv3, p3, smt3, pc, pr, lc, d, reverse):
    nb, seq, _ = cv3.shape
    nc = seq // CHUNK
    nctx = lc // CHUNK

    def ch(i):
        return _chunk_order(i, nctx, nc, reverse)

    def cspec(col):
        return pl.BlockSpec((nb, CHUNK, GW), lambda i: (0, ch(i), col))

    return pl.pallas_call(
        functools.partial(_gdn_body, reverse=reverse, d=d, nb=nb, group=GDN_GROUP),
        out_shape=jax.ShapeDtypeStruct((nb, seq, GW), f32),
        grid=(nc,),
        in_specs=[cspec(0), cspec(1), cspec(2),
                  pl.BlockSpec((nb, CHUNK, 128), lambda i: (0, ch(i), P_SMALL // 128)),
                  pl.BlockSpec((nb, 32, CHUNK), lambda i: (0, 0, ch(i))),
                  pl.BlockSpec((1, 2, 128), lambda i: (d, 0, 0)),
                  pl.BlockSpec((1, 32, 2), lambda i: (d, 0, 0))],
        out_specs=pl.BlockSpec((nb, CHUNK, GW), lambda i: (0, ch(i), 0)),
        scratch_shapes=[pltpu.VMEM((nb * GDN_HEADS, HEAD_DIM, HEAD_DIM), f32)],
        compiler_params=_cparams(("arbitrary",)),
        name="gdn_rev" if reverse else "gdn_fwd",
    )(cv3, cv3, cv3, p3, smt3, pc, pr)


def _rms_lanes(x, w, eps):
    return x * lax.rsqrt(jnp.mean(x * x, axis=-1, keepdims=True) + eps) * w


def _finish_body(hf, hb, rf, rb, sf, sb, gf, gb, gate, rg, cz, dz, xs, retw, ssdd, ssdw, gdnw, o_ref):
    g = gate[...]
    gelu = 0.5 * g * (1.0 + jnp.tanh(math.sqrt(2.0 / math.pi) * (g + 0.044715 * (g * g * g))))
    o_ref[:, 0:GW] = (gelu * (hf[...] + hb[...])).astype(bf16)

    ro = rf[...] + rb[...]
    rgate = _silu(rg[...])
    go = gf[...] + gb[...]
    ggate = _silu(dz[...])
    for h in range(RET_HEADS):
        lanes = slice(h * HEAD_DIM, (h + 1) * HEAD_DIM)
        o_ref[:, GW + h * HEAD_DIM:GW + (h + 1) * HEAD_DIM] = (
            rgate[:, lanes] * _rms_lanes(ro[:, lanes], retw[:, lanes], 1e-6)).astype(bf16)
        o_ref[:, 3 * GW + h * HEAD_DIM:3 * GW + (h + 1) * HEAD_DIM] = (
            _rms_lanes(go[:, lanes], gdnw[:, lanes], 1e-6) * ggate[:, lanes]).astype(bf16)

    sy = (sf[...] + sb[...] + ssdd[...] * xs[...]) * _silu(cz[...])
    o_ref[:, 2 * GW:3 * GW] = _rms_lanes(sy, ssdw[...], 1e-6).astype(bf16)


def _finish_call(scans, p, cv, retw, ssdd, ssdw, gdnw):
    n = p.shape[0]
    tm = ROW_TILE

    def rowspec(col):
        return pl.BlockSpec((tm, GW), lambda i: (i, col))

    vec = pl.BlockSpec((1, GW), lambda i: (0, 0))
    return pl.pallas_call(
        _finish_body,
        out_shape=jax.ShapeDtypeStruct((n, D_MODEL), bf16),
        grid=(n // tm,),
        in_specs=[rowspec(0)] * 8 + [rowspec(P_GATE // GW), rowspec(P_G // GW), rowspec(P_CZ // GW),
                                     rowspec(P_DZ // GW), rowspec(4), vec, vec, vec, vec],
        out_specs=pl.BlockSpec((tm, D_MODEL), lambda i: (i, 0)),
        compiler_params=_cparams(("arbitrary",)),
        name="mixer_finish",
    )(*scans, p, p, p, p, cv, retw, ssdd, ssdw, gdnw)


def _layer_norm_rows(t, w, b):
    mu = jnp.mean(t, axis=-1, keepdims=True)
    tc = t - mu
    var = jnp.mean(tc * tc, axis=-1, keepdims=True)
    return tc * lax.rsqrt(var + 1e-5) * w + b


HALF = D_MODEL // 2
HI_MASK = 0xFFFF0000


def _pack_bf16_pair(lo, hi):
    lo_bits = lax.bitcast_convert_type(lo.astype(bf16).astype(f32), jnp.uint32)
    hi_bits = lax.bitcast_convert_type(hi.astype(bf16).astype(f32), jnp.uint32)
    return hi_bits | (lo_bits >> 16)


def _unpack_bf16_pair(w):
    lo = lax.bitcast_convert_type(w << 16, f32)
    hi = lax.bitcast_convert_type(w & jnp.uint32(HI_MASK), f32)
    return lo, hi


TOKEN_ROWS = HALF // 128


def _store_token_tiles(ref, packed):
    m = packed.shape[0]
    for c in range(TOKEN_ROWS):
        ref[pl.ds(c, m, stride=TOKEN_ROWS), :] = packed[:, c * 128:(c + 1) * 128]


def _load_token_tiles(ref, m):
    return jnp.concatenate([ref[pl.ds(c, m, stride=TOKEN_ROWS), :] for c in range(TOKEN_ROWS)], axis=1)


def _split_bf16(x):
    hi = x.astype(bf16)
    return hi, (x - hi.astype(f32)).astype(bf16)


def _outproj_body(y_ref, w_ref, x_ref, g1_ref, lw_ref, lb_ref, sc_ref, sh_ref, rwh_ref, rwl_ref,
                  x1_ref, u_ref, lg_ref, *, alpha):
    y = jnp.dot(y_ref[...], w_ref[...], preferred_element_type=f32)
    x1 = _layer_norm_rows(alpha * x_ref[...] + g1_ref[0] * y, lw_ref[...], lb_ref[...])
    x1_ref[...] = x1
    u = x1 * (1.0 + sc_ref[0]) + sh_ref[0]
    _store_token_tiles(u_ref, _pack_bf16_pair(u[:, :HALF], u[:, HALF:]))
    uh, ul = _split_bf16(u)
    rwh = rwh_ref[...]
    lg = (jnp.dot(uh, rwh, preferred_element_type=f32) + jnp.dot(ul, rwh, preferred_element_type=f32)
          + jnp.dot(uh, rwl_ref[...], preferred_element_type=f32))
    lg_ref[...] = lg.T[:N_EXPERTS, :]


def _outproj_call(ycat, w_bf, x, g1, lw, lb, sc2, sh2, rw_hi, rw_lo, alpha):
    n, d = x.shape
    tm = ROW_TILE
    mod = pl.BlockSpec((1, 1, d), lambda i: (i, 0, 0))
    vec = pl.BlockSpec((1, d), lambda i: (0, 0))
    rws = pl.BlockSpec((d, 2 * N_EXPERTS), lambda i: (0, 0))
    return pl.pallas_call(
        functools.partial(_outproj_body, alpha=alpha),
        out_shape=(jax.ShapeDtypeStruct((n, d), f32), jax.ShapeDtypeStruct((n * TOKEN_ROWS, 128), jnp.uint32),
                   jax.ShapeDtypeStruct((N_EXPERTS, n), f32)),
        grid=(n // tm,),
        in_specs=[pl.BlockSpec((tm, d), lambda i: (i, 0)),
                  pl.BlockSpec((d, d), lambda i: (0, 0)),
                  pl.BlockSpec((tm, d), lambda i: (i, 0)),
                  mod, vec, vec, mod, mod, rws, rws],
        out_specs=(pl.BlockSpec((tm, d), lambda i: (i, 0)), pl.BlockSpec((tm * TOKEN_ROWS, 128), lambda i: (i, 0)),
                   pl.BlockSpec((N_EXPERTS, tm), lambda i: (0, i))),
        compiler_params=_cparams(("arbitrary",)),
        name="outproj_ln1",
    )(ycat, w_bf, x, g1, lw, lb, sc2, sh2, rw_hi, rw_lo)


EXPERT_TILE = 256


def _num_expert_tiles(n_pairs):
    return (n_pairs + N_EXPERTS * (EXPERT_TILE - 1) + EXPERT_TILE - 1) // EXPERT_TILE


def _route_rank_body(lg_ref, bias_ref, idx_ref, wt_ref, rank_ref, cnt_ref, carry_s):
    i = pl.program_id(0)

    @pl.when(i == 0)
    def _():
        carry_s[...] = jnp.zeros_like(carry_s)

    tn = lg_ref.shape[1]
    scores = _sigmoid(lg_ref[...])
    sel = scores + bias_ref[...]
    gsz = N_EXPERTS // N_EXPERT_GROUPS
    neg = -jnp.inf
    sel3 = sel.reshape(N_EXPERT_GROUPS, gsz, tn)
    io3 = lax.broadcasted_iota(jnp.int32, sel3.shape, 1)
    m1 = jnp.max(sel3, axis=1)
    first = jnp.min(jnp.where(sel3 == m1[:, None, :], io3, gsz), axis=1)
    m2 = jnp.max(jnp.where(io3 == first[:, None, :], neg, sel3), axis=1)
    gscore = m1 + m2
    iog = lax.broadcasted_iota(jnp.int32, gscore.shape, 0)
    gsel = None
    for _ in range(TOPK_GROUPS):
        gm = jnp.max(gscore, axis=0, keepdims=True)
        gi = jnp.min(jnp.where(gscore == gm, iog, N_EXPERT_GROUPS), axis=0, keepdims=True)
        hit = iog == gi
        gsel = hit if gsel is None else jnp.logical_or(gsel, hit)
        gscore = jnp.where(hit, neg, gscore)
    emask = jnp.broadcast_to(gsel[:, None, :], sel3.shape).reshape(N_EXPERTS, tn)
    cand = jnp.where(emask, sel, neg)
    ioe = lax.broadcasted_iota(jnp.int32, cand.shape, 0)
    idxs = []
    wts = []
    hits = []
    for _ in range(TOP_K):
        cm = jnp.max(cand, axis=0, keepdims=True)
        ci = jnp.min(jnp.where(cand == cm, ioe, N_EXPERTS), axis=0, keepdims=True)
        hit = ioe == ci
        idxs.append(ci)
        hits.append(hit)
        wts.append(jnp.sum(jnp.where(hit, scores, 0.0), axis=0, keepdims=True))
        cand = jnp.where(hit, neg, cand)
    wsum = wts[0]
    for w in wts[1:]:
        wsum = wsum + w
    idx_ref[...] = jnp.concatenate(idxs, axis=0)
    wnorm = jnp.concatenate([w / wsum * ROUTED_SCALE for w in wts], axis=0)
    wt_ref[...] = jnp.concatenate([wnorm, jnp.zeros((128 - TOP_K, tn), f32)], axis=0).T

    r_io = lax.broadcasted_iota(jnp.int32, (tn, tn), 0)
    c_io = lax.broadcasted_iota(jnp.int32, (tn, tn), 1)
    prefix = jnp.where(r_io <= c_io, 1.0, 0.0).astype(bf16)
    base = carry_s[...]
    ranks = []
    for hit in hits:
        onehot = jnp.where(hit, 1.0, 0.0)
        cum = jnp.dot(onehot.astype(bf16), prefix, preferred_element_type=f32)
        ranks.append(jnp.sum(onehot * (cum - 1.0 + base), axis=0, keepdims=True))
        base = base + cum[:, tn - 1:tn]
    carry_s[...] = base
    rank_ref[...] = jnp.concatenate(ranks, axis=0).astype(jnp.int32)
    cnt_ref[...] = jnp.broadcast_to(base, cnt_ref.shape)


def _route_rank_call(logits_t, bias):
    n = logits_t.shape[1]
    tn = ROW_TILE
    kt = pl.BlockSpec((TOP_K, tn), lambda i: (0, i))
    return pl.pallas_call(
        _route_rank_body,
        out_shape=(jax.ShapeDtypeStruct((TOP_K, n), jnp.int32),
                   jax.ShapeDtypeStruct((n, 128), f32),
                   jax.ShapeDtypeStruct((TOP_K, n), jnp.int32),
                   jax.ShapeDtypeStruct((N_EXPERTS, 128), f32)),
        grid=(n // tn,),
        in_specs=[pl.BlockSpec((N_EXPERTS, tn), lambda i: (0, i)),
                  pl.BlockSpec((N_EXPERTS, 1), lambda i: (0, 0))],
        out_specs=(kt, pl.BlockSpec((tn, 128), lambda i: (i, 0)), kt,
                   pl.BlockSpec((N_EXPERTS, 128), lambda i: (0, 0))),
        scratch_shapes=[pltpu.VMEM((N_EXPERTS, 1), f32)],
        compiler_params=_cparams(("arbitrary",)),
        name="router_rank",
    )(logits_t, bias)


def _expert_layout(counts, nt):
    tiles_e = (counts + EXPERT_TILE - 1) // EXPERT_TILE
    tend = jnp.cumsum(tiles_e)
    tstart = tend - tiles_e
    n_used = tend[-1:].astype(jnp.int32)
    tile = jnp.arange(nt, dtype=jnp.int32)
    te = jnp.minimum(jnp.sum((tend[None, :] <= tile[:, None]).astype(jnp.int32), axis=1), N_EXPERTS - 1)
    last_tile = jnp.where(tiles_e > 0, tend - 1, -1).astype(jnp.int32)
    pstart = (tstart * EXPERT_TILE).astype(f32)[:, None]
    return te.astype(jnp.int32), n_used, last_tile, pstart


def _dispatch_body(last_ref, nu_ref, u_ref, idx_ref, rank_ref, pstart_ref, pos_ref, xs_hbm,
                   zbuf, pos_s, zsem, psem, ssem):
    i = pl.program_id(0)
    tn = idx_ref.shape[1]
    tr = TOKEN_ROWS
    tile_rows = EXPERT_TILE * tr
    n_tiles = xs_hbm.shape[0] // tile_rows

    @pl.when(i == 0)
    def _():
        zbuf[...] = jnp.zeros_like(zbuf)

        def zero_tile(t):
            row = pl.multiple_of(t * tile_rows, tile_rows)
            return pltpu.make_async_copy(zbuf, xs_hbm.at[pl.ds(row, tile_rows)], zsem)

        for e in range(N_EXPERTS):
            @pl.when(last_ref[e] >= 0)
            def _(e=e):
                zero_tile(last_ref[e]).start()
        lax.fori_loop(nu_ref[0], n_tiles, lambda t, c: (zero_tile(t).start(), c)[1], 0)
        for e in range(N_EXPERTS):
            @pl.when(last_ref[e] >= 0)
            def _():
                zero_tile(0).wait()
        lax.fori_loop(nu_ref[0], n_tiles, lambda t, c: (zero_tile(0).wait(), c)[1], 0)

    ioe = lax.broadcasted_iota(jnp.int32, (N_EXPERTS, tn), 0)
    pstart = pstart_ref[...]
    pos = []
    for k in range(TOP_K):
        start_k = jnp.sum(jnp.where(ioe == idx_ref[k:k + 1, :], pstart, 0.0), axis=0, keepdims=True)
        pos.append(start_k.astype(jnp.int32) + rank_ref[k:k + 1, :])
    pos_ref[...] = jnp.concatenate(pos, axis=0)
    to_smem = pltpu.make_async_copy(pos_ref, pos_s, psem)
    to_smem.start()
    to_smem.wait()

    def per_token(t, carry):
        src = u_ref.at[pl.ds(pl.multiple_of(t * tr, tr), tr)]
        for k in range(TOP_K):
            row = pl.multiple_of(pos_s[k, t] * tr, tr)
            pltpu.make_async_copy(src, xs_hbm.at[pl.ds(row, tr)], ssem).start(priority=k % 2)
        return carry

    lax.fori_loop(0, tn, per_token, 0, unroll=4)
    for k in range(TOP_K):
        pltpu.make_async_copy(u_ref, xs_hbm.at[pl.ds(0, tn * tr)], ssem).wait()


def _dispatch_call(last_tile, n_used, u_pk, idx, rank, pstart, nt):
    n = idx.shape[1]
    tn = ROW_TILE
    tr = TOKEN_ROWS
    kt = pl.BlockSpec((TOP_K, tn), lambda i, last, nu: (0, i))
    grid_spec = pltpu.PrefetchScalarGridSpec(
        num_scalar_prefetch=2,
        grid=(n // tn,),
        in_specs=[pl.BlockSpec((tn * tr, 128), lambda i, last, nu: (i, 0)), kt, kt,
                  pl.BlockSpec((N_EXPERTS, 1), lambda i, last, nu: (0, 0))],
        out_specs=(kt, pl.BlockSpec(memory_space=pl.ANY)),
        scratch_shapes=[pltpu.VMEM((EXPERT_TILE * tr, 128), jnp.uint32),
                        pltpu.SMEM((TOP_K, tn), jnp.int32),
                        pltpu.SemaphoreType.DMA(()), pltpu.SemaphoreType.DMA(()), pltpu.SemaphoreType.DMA(())])
    return pl.pallas_call(
        _dispatch_body,
        out_shape=(jax.ShapeDtypeStruct((TOP_K, n), jnp.int32),
                   jax.ShapeDtypeStruct((nt * EXPERT_TILE * tr, 128), jnp.uint32)),
        grid_spec=grid_spec,
        compiler_params=_cparams(("arbitrary",)),
        name="dispatch",
    )(last_tile, n_used, u_pk, idx, rank, pstart)


def _sorted_experts_body(te_ref, nu_ref, x_ref, w1_ref, w3_ref, w2_ref, o_ref, w1_s, w3_s, w2_s):
    i = pl.program_id(0)
    nu = nu_ref[0]
    tm = EXPERT_TILE

    @pl.when(i < nu)
    def _():
        prev = te_ref[jnp.maximum(i - 1, 0)]

        @pl.when(jnp.logical_or(i == 0, te_ref[i] != prev))
        def _():
            w1_s[...] = w1_ref[0, 0].astype(bf16)
            w3_s[...] = w3_ref[0, 0].astype(bf16)
            w2_s[...] = w2_ref[0, 0].astype(bf16)

        lo, hi = _unpack_bf16_pair(_load_token_tiles(x_ref, tm))
        lo = lo.astype(bf16)
        hi = hi.astype(bf16)
        h1 = (jnp.dot(lo, w1_s[:HALF, :], preferred_element_type=f32)
              + jnp.dot(hi, w1_s[HALF:, :], preferred_element_type=f32))
        h3 = (jnp.dot(lo, w3_s[:HALF, :], preferred_element_type=f32)
              + jnp.dot(hi, w3_s[HALF:, :], preferred_element_type=f32))
        y = jnp.dot((_silu(h1) * h3).astype(bf16), w2_s[...], preferred_element_type=f32)
        _store_token_tiles(o_ref, _pack_bf16_pair(y[:, :HALF], y[:, HALF:]))

    @pl.when(i >= nu)
    def _():
        o_ref[...] = jnp.zeros_like(o_ref)


def _sorted_experts_call(te, n_used, xs, w1, w3, w2, layer):
    nt = te.shape[0]
    _, ne, d, fe = w1.shape
    tm = EXPERT_TILE
    rows = tm * TOKEN_ROWS

    def wmap(i, te_r, nu_r):
        return (layer, te_r[jnp.minimum(i, nu_r[0] - 1)], 0, 0)

    grid_spec = pltpu.PrefetchScalarGridSpec(
        num_scalar_prefetch=2,
        grid=(nt,),
        in_specs=[pl.BlockSpec((rows, 128), lambda i, te_r, nu_r: (jnp.minimum(i, nu_r[0] - 1), 0)),
                  pl.BlockSpec((1, 1, d, fe), wmap), pl.BlockSpec((1, 1, d, fe), wmap),
                  pl.BlockSpec((1, 1, fe, d), wmap)],
        out_specs=pl.BlockSpec((rows, 128), lambda i, te_r, nu_r: (i, 0)),
        scratch_shapes=[pltpu.VMEM((d, fe), bf16), pltpu.VMEM((d, fe), bf16), pltpu.VMEM((fe, d), bf16)])
    return pl.pallas_call(
        _sorted_experts_body,
        out_shape=jax.ShapeDtypeStruct(xs.shape, jnp.uint32),
        grid_spec=grid_spec,
        compiler_params=_cparams(("arbitrary",)),
        name="experts_sorted",
    )(te, n_used, xs, w1, w3, w2)


def _gather_combine_body(pos_ref, wt_ref, u_ref, x_ref, g2_ref, lw_ref, lb_ref, s1_ref, s3_ref, s2_ref, ys_hbm,
                         o_ref, pos_s, gbuf, psem, gsem, *, alpha):
    tn = x_ref.shape[0]
    tr = TOKEN_ROWS
    to_smem = pltpu.make_async_copy(pos_ref, pos_s, psem)
    to_smem.start()
    to_smem.wait()

    def per_token(t, carry):
        dst_row = pl.multiple_of(t * tr, tr)
        for k in range(TOP_K):
            row = pl.multiple_of(pos_s[k, t] * tr, tr)
            pltpu.make_async_copy(ys_hbm.at[pl.ds(row, tr)], gbuf.at[k, pl.ds(dst_row, tr)],
                                  gsem).start(priority=k % 2)
        return carry

    lax.fori_loop(0, tn, per_token, 0, unroll=4)

    ulo, uhi = _unpack_bf16_pair(_load_token_tiles(u_ref, tn))
    ulo = ulo.astype(bf16)
    uhi = uhi.astype(bf16)
    a1 = (jnp.dot(ulo, s1_ref[:HALF, :], preferred_element_type=f32)
          + jnp.dot(uhi, s1_ref[HALF:, :], preferred_element_type=f32))
    a3 = (jnp.dot(ulo, s3_ref[:HALF, :], preferred_element_type=f32)
          + jnp.dot(uhi, s3_ref[HALF:, :], preferred_element_type=f32))
    shared = jnp.dot((_silu(a1) * a3).astype(bf16), s2_ref[...], preferred_element_type=f32)

    for k in range(TOP_K):
        pltpu.make_async_copy(ys_hbm.at[pl.ds(0, tn * tr)], gbuf.at[k], gsem).wait()
    wt = wt_ref[...]
    lo = None
    hi = None
    for k in range(TOP_K):
        lo_k, hi_k = _unpack_bf16_pair(_load_token_tiles(gbuf.at[k], tn))
        gate = wt[:, k:k + 1]
        lo = gate * lo_k if lo is None else lo + gate * lo_k
        hi = gate * hi_k if hi is None else hi + gate * hi_k
    f = jnp.concatenate([lo, hi], axis=1) + shared
    o_ref[...] = _layer_norm_rows(alpha * x_ref[...] + g2_ref[0] * f, lw_ref[...], lb_ref[...])


def _gather_combine_call(pos, wt_tm, u_pk, x1, g2, lw, lb, s1, s3, s2, ys, alpha):
    n, d = x1.shape
    tn = ROW_TILE
    tr = TOKEN_ROWS
    fe = s1.shape[1]
    row = pl.BlockSpec((tn, d), lambda i: (i, 0))
    vec = pl.BlockSpec((1, d), lambda i: (0, 0))
    return pl.pallas_call(
        functools.partial(_gather_combine_body, alpha=alpha),
        out_shape=jax.ShapeDtypeStruct((n, d), f32),
        grid=(n // tn,),
        in_specs=[pl.BlockSpec((TOP_K, tn), lambda i: (0, i)),
                  pl.BlockSpec((tn, 128), lambda i: (i, 0)),
                  pl.BlockSpec((tn * tr, 128), lambda i: (i, 0)), row,
                  pl.BlockSpec((1, 1, d), lambda i: (i, 0, 0)), vec, vec,
                  pl.BlockSpec((d, fe), lambda i: (0, 0)), pl.BlockSpec((d, fe), lambda i: (0, 0)),
                  pl.BlockSpec((fe, d), lambda i: (0, 0)),
                  pl.BlockSpec(memory_space=pl.ANY)],
        out_specs=row,
        scratch_shapes=[pltpu.SMEM((TOP_K, tn), jnp.int32),
                        pltpu.VMEM((TOP_K, tn * tr, 128), jnp.uint32),
                        pltpu.SemaphoreType.DMA(()), pltpu.SemaphoreType.DMA(())],
        compiler_params=_cparams(("arbitrary",)),
        name="gather_combine_ln2",
    )(pos, wt_tm, u_pk, x1, g2, lw, lb, s1, s3, s2, ys)


def _block_diag(w):
    g, n, _ = w.shape
    eye = jnp.eye(g, dtype=w.dtype)
    return (eye[:, None, :, None] * w[:, :, None, :]).reshape(g * n, g * n)


def _rope_tables(seq, lc):
    half = HEAD_DIM // 2
    t = jnp.arange(seq - lc, dtype=f32)
    inv = ROPE_BASE ** (-jnp.arange(0, half, 2, dtype=f32) / half)
    ar = jnp.floor(t / GRID_W)[:, None] * inv[None, :]
    ac = (t - jnp.floor(t / GRID_W) * GRID_W)[:, None] * inv[None, :]
    cos = jnp.concatenate([jnp.cos(ar), jnp.cos(ar), jnp.cos(ac), jnp.cos(ac)], axis=1)
    sin = jnp.concatenate([-jnp.sin(ar), jnp.sin(ar), -jnp.sin(ac), jnp.sin(ac)], axis=1)
    cos = jnp.concatenate([jnp.ones((lc, HEAD_DIM), f32), cos], axis=0)
    sin = jnp.concatenate([jnp.zeros((lc, HEAD_DIM), f32), sin], axis=0)
    return cos, sin


def _pad_lanes(v, offset, width=128):
    return jnp.zeros((width,), f32).at[offset:offset + v.shape[0]].set(v)


def kernel(x, c, ctx, c_ctx, w_ada, b_ada, w_in, lru_conv_w, lru_conv_b, lru_gate_a_w, lru_gate_a_b,
           lru_gate_x_w, lru_gate_x_b, lru_lambda, ret_log_decay, ret_norm_w, ssd_conv_w, ssd_conv_b,
           ssd_dt_bias, ssd_a_log, ssd_d, ssd_norm_w, gdn_conv_w, gdn_dt_bias, gdn_a_log, gdn_norm_w,
           w_out, ln1_w, ln1_b, router_w, router_bias, exp_w1, exp_w3, exp_w2, sh_w1, sh_w3, sh_w2,
           ln2_w, ln2_b):
    nb, lat, d = x.shape
    lc = ctx.shape[1]
    depth = w_ada.shape[0]
    seq = lc + lat
    n = nb * seq
    ntile = n // ROW_TILE
    tiles_per_seq = seq // ROW_TILE
    alpha = (2 * depth) ** 0.25
    assert d == D_MODEL and lc == ROW_TILE and lat % ROW_TILE == 0 and nb <= 7

    xall = jnp.concatenate([ctx, x], axis=1).reshape(n, d)

    svec = jnp.zeros((8, d), f32).at[0].set(c_ctx).at[1:1 + nb].set(c)
    modtab = _ada_call(svec, w_ada, b_ada).reshape(depth, 8, 6, d)
    tile = jnp.arange(ntile)
    tile_row = jnp.where(tile % tiles_per_seq == 0, 0, 1 + tile // tiles_per_seq)
    cos, sin = _rope_tables(seq, lc)

    for l in range(depth):
        mods = [modtab[l, :, k, :][tile_row][:, None, :] for k in range(6)]
        sh1, sc1, g1, sh2, sc2, g2 = mods

        wl = w_in[l]
        w_re = jnp.concatenate([wl[:, :3584], wl[:, 5896:6408], wl[:, 4360:5896], wl[:, 3584:4352],
                                wl[:, 4352:4360], wl[:, 6408:6424],
                                jnp.zeros((d, IN_COLS_PAD - 6424), f32)], axis=1).astype(bf16)
        p = _inproj_call(xall, sc1, sh1, w_re)
        smt = p[:, P_SMALL:P_SMALL + 32].T

        wcat = jnp.concatenate([gdn_conv_w[l], lru_conv_w[l], ssd_conv_w[l]], axis=1)
        bcat = jnp.concatenate([jnp.zeros((3 * GW,), f32), lru_conv_b[l], ssd_conv_b[l]])[None, :]
        cv = _conv_call(p, wcat, bcat, nb, seq, lc)

        wg = jnp.stack([jnp.concatenate([_block_diag(lru_gate_a_w[l, dd]), _block_diag(lru_gate_x_w[l, dd])],
                                        axis=1) for dd in range(2)]).astype(bf16)
        bg = jnp.concatenate([lru_gate_a_b[l], lru_gate_x_b[l]], axis=1)[:, None, :]
        sp = jax.nn.softplus(-lru_lambda[l])[:, None, :]
        hs = [_lru_call(cv, wg, bg, sp, nb, seq, lc, dd, bool(dd)) for dd in range(2)]

        lg = -jnp.exp(ret_log_decay[l].astype(f32))
        rs = [_ret_call(p, lg[dd], cos, sin, nb, seq, lc, bool(dd)) for dd in range(2)]

        a_ssd = -jnp.exp(ssd_a_log[l])
        pc = jnp.stack([jnp.stack([_pad_lanes(ssd_dt_bias[l, dd], 0), _pad_lanes(a_ssd[dd], 0)])
                        for dd in range(2)])
        pr = jnp.stack([jnp.stack([ssd_dt_bias[l, dd], a_ssd[dd]], axis=1) for dd in range(2)])
        ss = [_ssd_call(cv, p, smt, pc, pr, nb, seq, lc, dd, bool(dd)) for dd in range(2)]

        a_gdn = -jnp.exp(gdn_a_log[l])
        pcg = jnp.stack([jnp.stack([_pad_lanes(gdn_dt_bias[l, dd], 8 + 4 * dd),
                                    _pad_lanes(a_gdn[dd], 8 + 4 * dd)]) for dd in range(2)])
        prg = jnp.stack([jnp.stack([_pad_lanes(gdn_dt_bias[l, dd], 8 + 4 * dd, 32),
                                    _pad_lanes(a_gdn[dd], 8 + 4 * dd, 32)], axis=1) for dd in range(2)])
        p3 = p.reshape(nb, seq, IN_COLS_PAD)
        cv3 = cv.reshape(nb, seq, CV_COLS)
        smt3 = jnp.transpose(p3[:, :, P_SMALL:P_SMALL + 32], (0, 2, 1))
        gs = [_gdn_call(cv3, p3, smt3, pcg, prg, lc, dd, bool(dd)).reshape(n, GW) for dd in range(2)]

        ycat = _finish_call(hs + rs + ss + gs, p, cv, ret_norm_w[l][None, :],
                            jnp.repeat(ssd_d[l], SSD_HEAD_DIM)[None, :], ssd_norm_w[l][None, :],
                            jnp.tile(gdn_norm_w[l], GDN_HEADS)[None, :])

        rw = jnp.concatenate([router_w[l], jnp.zeros((d, N_EXPERTS), f32)], axis=1)
        rw_hi = rw.astype(bf16)
        rw_lo = (rw - rw_hi.astype(f32)).astype(bf16)
        x1, u_pk, logits_t = _outproj_call(ycat, w_out[l].astype(bf16), xall, g1, ln1_w[l][None, :],
                                           ln1_b[l][None, :], sc2, sh2, rw_hi, rw_lo, alpha)
        idx, wt_tm, rank, cnt = _route_rank_call(logits_t, router_bias[l][:, None])
        nt = _num_expert_tiles(TOP_K * n)
        te, n_used, last_tile, pstart = _expert_layout(cnt[:, 0].astype(jnp.int32), nt)
        pos, xs = _dispatch_call(last_tile, n_used, u_pk, idx, rank, pstart, nt)
        ys = _sorted_experts_call(te, n_used, xs, exp_w1, exp_w3, exp_w2, l)
        xall = _gather_combine_call(pos, wt_tm, u_pk, x1, g2, ln2_w[l][None, :], ln2_b[l][None, :],
                                    sh_w1[l].astype(bf16), sh_w3[l].astype(bf16), sh_w2[l].astype(bf16),
                                    ys, alpha)

    return xall.reshape(nb, seq, d)[:, lc:, :]
```

```python
import functools
import math

import jax
import jax.numpy as jnp
from jax import lax
from jax.experimental import pallas as pl
from jax.experimental.pallas import tpu as pltpu

f32 = jnp.float32
bf16 = jnp.bfloat16
HIGHEST = lax.Precision.HIGHEST

D_MODEL = 2048
GW = 512
GRID_W = 64
CONV_K = 4
LRU_C = 8.0
RET_HEADS = 4
HEAD_DIM = 128
ROPE_BASE = 10000.0
SSD_HEADS = 8
SSD_HEAD_DIM = 64
SSD_STATE = 64
GDN_HEADS = 4
N_EXPERTS = 64
TOP_K = 8
N_EXPERT_GROUPS = 8
TOPK_GROUPS = 4
D_EXPERT = 256
ROUTED_SCALE = 2.5

ROW_TILE = 256
CHUNK = 128
IN_COLS_PAD = 6528
CV_COLS = 2816
VMEM_LIMIT = 56 * 1024 * 1024

P_XB, P_GATE, P_Q, P_K, P_V, P_G, P_CZ, P_DZ, P_DQKV, P_XBC, P_SMALL = (
    0, 512, 1024, 1536, 2048, 2560, 3072, 3584, 4096, 5632, 6400)


def _cparams(sem):
    return pltpu.CompilerParams(dimension_semantics=sem, vmem_limit_bytes=VMEM_LIMIT)


def _sigmoid(x):
    return 1.0 / (1.0 + jnp.exp(-x))


def _silu(x):
    return x * _sigmoid(x)


def _softplus(x):
    return jnp.maximum(x, 0.0) + jnp.log(1.0 + jnp.exp(-jnp.abs(x)))


def _dot(a, b):
    return jnp.dot(a.astype(bf16), b.astype(bf16), preferred_element_type=f32)


def _dot_nt(a, b):
    return lax.dot_general(a.astype(bf16), b.astype(bf16), (((1,), (1,)), ((), ())),
                           preferred_element_type=f32)


def _dot_tn(a, b):
    return lax.dot_general(a.astype(bf16), b.astype(bf16), (((0,), (0,)), ((), ())),
                           preferred_element_type=f32)


def _chunk_order(i, n_ctx, n_all, reverse):
    if not reverse:
        return i
    return jnp.where(i < n_ctx, n_ctx - 1 - i, n_all + n_ctx - 1 - i)


def _ada_body(s_ref, w_ref, b_ref, o_ref):
    xh, xl = _split_bf16(_silu(s_ref[...]))
    wh, wl = _split_bf16(w_ref[0])
    acc = jnp.dot(xh, wh, preferred_element_type=f32) + jnp.dot(xl, wh, preferred_element_type=f32)
    o_ref[0] = acc + jnp.dot(xh, wl, preferred_element_type=f32) + b_ref[0]


def _ada_call(svec, w_ada, b_ada):
    nl, d, d6 = w_ada.shape
    tn = 1024
    return pl.pallas_call(
        _ada_body,
        out_shape=jax.ShapeDtypeStruct((nl, 8, d6), f32),
        grid=(nl, d6 // tn),
        in_specs=[pl.BlockSpec((8, d), lambda l, j: (0, 0)),
                  pl.BlockSpec((1, d, tn), lambda l, j: (l, 0, j)),
                  pl.BlockSpec((1, 1, tn), lambda l, j: (l, 0, j))],
        out_specs=pl.BlockSpec((1, 8, tn), lambda l, j: (l, 0, j)),
        compiler_params=_cparams(("arbitrary", "arbitrary")),
        name="ada",
    )(svec, w_ada, b_ada.reshape(nl, 1, d6))


def _inproj_body(x_ref, sc_ref, sh_ref, w_ref, o_ref, *, sub):
    for s in range(sub):
        rows = slice(s * ROW_TILE, (s + 1) * ROW_TILE)
        xm = x_ref[rows, :] * (1.0 + sc_ref[s]) + sh_ref[s]
        o_ref[rows, :] = _dot(xm, w_ref[...])


def _inproj_call(x, sc, sh, w_bf):
    n, d = x.shape
    sub = 2
    tm = sub * ROW_TILE
    tn = IN_COLS_PAD // 3
    return pl.pallas_call(
        functools.partial(_inproj_body, sub=sub),
        out_shape=jax.ShapeDtypeStruct((n, IN_COLS_PAD), f32),
        grid=(3, n // tm),
        in_specs=[pl.BlockSpec((tm, d), lambda j, i: (i, 0)),
                  pl.BlockSpec((sub, 1, d), lambda j, i: (i, 0, 0)),
                  pl.BlockSpec((sub, 1, d), lambda j, i: (i, 0, 0)),
                  pl.BlockSpec((d, tn), lambda j, i: (0, j))],
        out_specs=pl.BlockSpec((tm, tn), lambda j, i: (i, j)),
        compiler_params=_cparams(("arbitrary", "arbitrary")),
        name="inproj",
    )(x, sc, sh, w_bf)


def _conv_body(x_ref, w_ref, b_ref, o_ref, *, seq, lc):
    j = pl.program_id(1)
    w = w_ref[...]
    bias = b_ref[...]
    use_act = jnp.logical_or(j < 6, j >= 8)
    row = lax.broadcasted_iota(jnp.int32, (CHUNK, 1), 0)

    def chunk(c, carry):
        r0 = pl.multiple_of(c * CHUNK, CHUNK)
        cur = x_ref[pl.ds(r0, CHUNK), :]
        prev = x_ref[pl.ds(pl.multiple_of(jnp.maximum(r0 - 8, 0), 8), 8), :]
        nxt = x_ref[pl.ds(pl.multiple_of(jnp.minimum(r0 + CHUNK, seq - 8), 8), 8), :]
        ext = jnp.concatenate([prev, cur, nxt], axis=0)
        t = r0 + row
        s0 = jnp.where(r0 < lc, 0, lc)
        s1 = jnp.where(r0 < lc, lc, seq)
        acc = jnp.zeros_like(cur) + bias
        for k in range(CONV_K):
            o = k - CONV_K // 2
            seg = ext[8 + o:8 + o + CHUNK]
            valid = jnp.logical_and(t + o >= s0, t + o < s1)
            acc = acc + jnp.where(valid, seg, 0.0) * w[k:k + 1, :]
        o_ref[pl.ds(r0, CHUNK), :] = jnp.where(use_act, _silu(acc), acc)
        return carry

    lax.fori_loop(0, seq // CHUNK, chunk, 0)


def _conv_call(p, wcat, bcat, nb, seq, lc):
    n = p.shape[0]
    cb = 256

    def in_map(b, j):
        return (b, jnp.where(j < 6, P_DQKV // cb + j, jnp.where(j < 8, j - 6, P_XBC // cb - 8 + j)))

    return pl.pallas_call(
        functools.partial(_conv_body, seq=seq, lc=lc),
        out_shape=jax.ShapeDtypeStruct((n, CV_COLS), f32),
        grid=(nb, CV_COLS // cb),
        in_specs=[pl.BlockSpec((seq, cb), in_map),
                  pl.BlockSpec((CONV_K, cb), lambda b, j: (0, j)),
                  pl.BlockSpec((1, cb), lambda b, j: (0, j))],
        out_specs=pl.BlockSpec((seq, cb), lambda b, j: (b, j)),
        compiler_params=_cparams(("arbitrary", "arbitrary")),
        name="dwconv",
    )(p, wcat, bcat)


def _lru_body(xc_ref, wg_ref, bg_ref, sp_ref, o_ref, a_s, b_s, h_s, *, reverse, nb):
    i = pl.program_id(0)

    @pl.when(i == 0)
    def _():
        h_s[...] = jnp.zeros_like(h_s)

    for b in range(nb):
        xc = xc_ref[b]
        gates = _dot(xc, wg_ref[0]) + bg_ref[0]
        r = _sigmoid(gates[:, :GW])
        ig = _sigmoid(gates[:, GW:])
        a = jnp.exp(-LRU_C * r * sp_ref[0])
        a_s[b] = a
        b_s[b] = jnp.sqrt(1.0 - a * a) * (ig * xc)

    def step(jj, hs):
        t = (ROW_TILE - 1 - jj) if reverse else jj
        out = []
        for b in range(nb):
            h = a_s[b, pl.ds(t, 1), :] * hs[b] + b_s[b, pl.ds(t, 1), :]
            o_ref[b, pl.ds(t, 1), :] = h
            out.append(h)
        return tuple(out)

    hs = lax.fori_loop(0, ROW_TILE, step, tuple(h_s[b] for b in range(nb)), unroll=8)
    for b in range(nb):
        h_s[b] = hs[b]


def _lru_call(cv3, wg, bg, sp, lc, d, reverse):
    nb, seq, _ = cv3.shape
    nt = seq // ROW_TILE
    nctx = lc // ROW_TILE

    def ch(i):
        return _chunk_order(i, nctx, nt, reverse)

    return pl.pallas_call(
        functools.partial(_lru_body, reverse=reverse, nb=nb),
        out_shape=jax.ShapeDtypeStruct((nb, seq, GW), f32),
        grid=(nt,),
        in_specs=[pl.BlockSpec((nb, ROW_TILE, GW), lambda i: (0, ch(i), 3)),
                  pl.BlockSpec((1, GW, 2 * GW), lambda i: (d, 0, 0)),
                  pl.BlockSpec((1, 1, 2 * GW), lambda i: (d, 0, 0)),
                  pl.BlockSpec((1, 1, GW), lambda i: (d, 0, 0))],
        out_specs=pl.BlockSpec((nb, ROW_TILE, GW), lambda i: (0, ch(i), 0)),
        scratch_shapes=[pltpu.VMEM((nb, ROW_TILE, GW), f32), pltpu.VMEM((nb, ROW_TILE, GW), f32),
                        pltpu.VMEM((nb, 1, GW), f32)],
        compiler_params=_cparams(("arbitrary",)),
        name="lru_rev" if reverse else "lru_fwd",
    )(cv3, wg, bg, sp)


def _tri_masks(reverse):
    ii = lax.broadcasted_iota(jnp.int32, (CHUNK, CHUNK), 0)
    jj = lax.broadcasted_iota(jnp.int32, (CHUNK, CHUNK), 1)
    if reverse:
        return jj >= ii, jj > ii, ii, jj
    return ii >= jj, ii > jj, ii, jj


def _rope(x, cos, sin):
    lane = lax.broadcasted_iota(jnp.int32, x.shape, 1)
    swapped = jnp.where(lane % 64 < 32, pltpu.roll(x, 96, 1), pltpu.roll(x, 32, 1))
    return x * cos + swapped * sin


def _each(fn, *lists):
    return [fn(*args) for args in zip(*lists)]


def _ret_body(lg_ref, q_ref, k_ref, v_ref, cos_ref, sin_ref, o_ref, s_s, *, reverse, nb):
    i = pl.program_id(0)

    @pl.when(i == 0)
    def _():
        s_s[...] = jnp.zeros_like(s_s)

    mask, _, ii, jj = _tri_masks(reverse)
    diff = ((jj - ii) if reverse else (ii - jj)).astype(f32)
    col = lax.broadcasted_iota(jnp.int32, (CHUNK, 1), 0).astype(f32)
    cos = cos_ref[...]
    sin = sin_ref[...]
    dec, eq, ek, ec = [], [], [], []
    for h in range(RET_HEADS):
        lg = lg_ref[h]
        dec.append(jnp.where(mask, jnp.exp(lg * diff), 0.0))
        eq.append(jnp.exp(lg * ((CHUNK - col) if reverse else (col + 1.0))))
        ek.append(jnp.exp(lg * (col if reverse else (CHUNK - 1.0 - col))))
        ec.append(jnp.exp(lg * CHUNK))

    def lanes(h):
        return slice(h * HEAD_DIM, (h + 1) * HEAD_DIM)

    chains = [(b, h) for b in range(nb) for h in range(RET_HEADS)]
    hd = [h for _, h in chains]
    qh = [_rope(q_ref[b, :, lanes(h)], cos, sin) * (HEAD_DIM ** -0.5) for b, h in chains]
    kh = [_rope(k_ref[b, :, lanes(h)], cos, sin) for b, h in chains]
    vh = [v_ref[b, :, lanes(h)] for b, h in chains]
    st = [s_s[b * RET_HEADS + h] for b, h in chains]
    sc = _each(lambda q, k, h: _dot_nt(q, k) * dec[h], qh, kh, hd)
    inter = _each(lambda q, s, h: _dot(q * eq[h], s), qh, st, hd)
    ys = _each(lambda s, v, y0: y0 + _dot(s, v), sc, vh, inter)
    new_s = _each(lambda s, k, v, h: ec[h] * s + _dot_tn(k * ek[h], v), st, kh, vh, hd)
    for (b, h), y, s in zip(chains, ys, new_s):
        o_ref[b, :, lanes(h)] = y
        s_s[b * RET_HEADS + h] = s


def _ret_call(p3, lg, cos, sin, lc, reverse):
    nb, seq, _ = p3.shape
    nc = seq // CHUNK
    nctx = lc // CHUNK

    def ch(i):
        return _chunk_order(i, nctx, nc, reverse)

    def pspec(col):
        return pl.BlockSpec((nb, CHUNK, GW), lambda i: (0, ch(i), col // GW))

    return pl.pallas_call(
        functools.partial(_ret_body, reverse=reverse, nb=nb),
        out_shape=jax.ShapeDtypeStruct((nb, seq, GW), f32),
        grid=(nc,),
        in_specs=[pl.BlockSpec(memory_space=pltpu.SMEM),
                  pspec(P_Q), pspec(P_K), pspec(P_V),
                  pl.BlockSpec((CHUNK, HEAD_DIM), lambda i: (ch(i), 0)),
                  pl.BlockSpec((CHUNK, HEAD_DIM), lambda i: (ch(i), 0))],
        out_specs=pl.BlockSpec((nb, CHUNK, GW), lambda i: (0, ch(i), 0)),
        scratch_shapes=[pltpu.VMEM((nb * RET_HEADS, HEAD_DIM, HEAD_DIM), f32)],
        compiler_params=_cparams(("arbitrary",)),
        name="ret_rev" if reverse else "ret_fwd",
    )(lg, p3, p3, p3, cos, sin)


def _cumsums(la_col, la_row, reverse):
    mask, _, ii, jj = _tri_masks(reverse)
    m_col = jnp.where(mask, 1.0, 0.0)
    m_row = jnp.where(jnp.logical_not(mask) | (ii == jj), 1.0, 0.0)
    cum_col = jnp.dot(m_col, la_col, precision=HIGHEST, preferred_element_type=f32)
    cum_row = jnp.dot(la_row, m_row, precision=HIGHEST, preferred_element_type=f32)
    return cum_col, cum_row, mask


def _ssd_body(xs_ref, bc_ref, sm_ref, smt_ref, pc_ref, pr_ref, o_ref, s_s, *, reverse, nb):
    i = pl.program_id(0)

    @pl.when(i == 0)
    def _():
        s_s[...] = jnp.zeros_like(s_s)

    lane = lax.broadcasted_iota(jnp.int32, (1, 128), 1)
    mask, _, _, _ = _tri_masks(reverse)
    prep = []
    for b in range(nb):
        dt_col = _softplus(sm_ref[b] + pc_ref[0, 0:1, :])
        la_col = pc_ref[0, 1:2, :] * dt_col
        la_row = pr_ref[0, :, 1:2] * _softplus(smt_ref[b, 0:SSD_HEADS, :] + pr_ref[0, :, 0:1])
        cum_col, cum_row, _ = _cumsums(la_col, la_row, reverse)
        tot = cum_col[0:1, :] if reverse else cum_col[CHUNK - 1:CHUNK, :]
        bm = bc_ref[b, :, :128]
        cm = bc_ref[b, :, 128:]
        grp = []
        for g in range(2):
            gmask = (lane // SSD_STATE) == g
            grp.append((jnp.where(gmask, cm, 0.0), jnp.where(gmask, bm, 0.0)))
        prep.append((dt_col, cum_col, cum_row, tot, grp))

    scores = {(b, g): _dot_nt(prep[b][4][g][0], prep[b][4][g][1]) for b in range(nb) for g in range(2)}
    heads = [(b, h) for b in range(nb) for h in range(SSD_HEADS)]

    def head_inputs(b, h):
        dt_col, cum_col, cum_row, tot, grp = prep[b]
        pp, hh, g = h // 2, h % 2, h // (SSD_HEADS // 2)
        hmask = (lane // SSD_HEAD_DIM) == hh
        cc = cum_col[:, h:h + 1]
        dec = jnp.where(mask, jnp.exp(cc - cum_row[h:h + 1, :]), 0.0)
        vh = jnp.where(hmask, xs_ref[b, :, pp * 128:(pp + 1) * 128] * dt_col[:, h:h + 1], 0.0)
        sth = jnp.where(hmask, s_s[b * 4 + pp], 0.0)
        return cc, dec, vh, sth, tot[:, h:h + 1], grp[g][0], grp[g][1], scores[(b, g)]

    ins = [head_inputs(b, h) for b, h in heads]
    intra = [_dot(sc * dec, vh) for _, dec, vh, _, _, _, _, sc in ins]
    inter = [_dot(cmg * jnp.exp(cc), sth) for cc, _, _, sth, _, cmg, _, _ in ins]
    upd = [jnp.exp(th) * sth + _dot_tn(bmg * jnp.exp(th - cc), vh) for cc, _, vh, sth, th, _, bmg, _ in ins]
    for j in range(0, len(heads), 2):
        b, h = heads[j]
        pp = h // 2
        o_ref[b, :, pp * 128:(pp + 1) * 128] = (intra[j] + inter[j]) + (intra[j + 1] + inter[j + 1])
        s_s[b * 4 + pp] = upd[j] + upd[j + 1]


def _ssd_call(cv3, p3, smt3, pc, pr, lc, d, reverse):
    nb, seq, _ = cv3.shape
    nc = seq // CHUNK
    nctx = lc // CHUNK

    def ch(i):
        return _chunk_order(i, nctx, nc, reverse)

    return pl.pallas_call(
        functools.partial(_ssd_body, reverse=reverse, nb=nb),
        out_shape=jax.ShapeDtypeStruct((nb, seq, GW), f32),
        grid=(nc,),
        in_specs=[pl.BlockSpec((nb, CHUNK, GW), lambda i: (0, ch(i), 4)),
                  pl.BlockSpec((nb, CHUNK, 256), lambda i: (0, ch(i), 10)),
                  pl.BlockSpec((nb, CHUNK, 128), lambda i: (0, ch(i), P_SMALL // 128)),
                  pl.BlockSpec((nb, 32, CHUNK), lambda i: (0, 0, ch(i))),
                  pl.BlockSpec((1, 2, 128), lambda i: (d, 0, 0)),
                  pl.BlockSpec((1, 8, 2), lambda i: (d, 0, 0))],
        out_specs=pl.BlockSpec((nb, CHUNK, GW), lambda i: (0, ch(i), 0)),
        scratch_shapes=[pltpu.VMEM((nb * 4, 128, 128), f32)],
        compiler_params=_cparams(("arbitrary",)),
        name="ssd_rev" if reverse else "ssd_fwd",
    )(cv3, cv3, p3, smt3, pc, pr)


TRI_BASE = 8
GDN_GROUP = 16


def _unit_tri_inverse_minus_eye(ms, ii, jj):
    base = (ii // TRI_BASE) == (jj // TRI_BASE)
    pws = _each(lambda m: jnp.where(base, -m, 0.0), ms)
    accs = pws
    size = 2
    while size < TRI_BASE:
        pws = _each(lambda p: _dot(p, p), pws)
        accs = _each(lambda a, p: a + p + _dot(a, p), accs, pws)
        size *= 2
    half = TRI_BASE
    while half < CHUNK:
        off = jnp.logical_and((ii // (2 * half)) == (jj // (2 * half)), (ii // half) != (jj // half))
        mos = _each(lambda m: jnp.where(off, m, 0.0), ms)
        xs = _each(lambda a, mo: mo + _dot(a, mo), accs, mos)
        accs = _each(lambda a, x: a - x - _dot(x, a), accs, xs)
        half *= 2
    return accs


def _gdn_body(q_ref, k_ref, v_ref, sm_ref, smt_ref, pc_ref, pr_ref, o_ref, s_s, *, reverse, d, nb, group):
    i = pl.program_id(0)

    @pl.when(i == 0)
    def _():
        s_s[...] = jnp.zeros_like(s_s)

    mask, strict, blk_i, blk_j = _tri_masks(reverse)
    per_b = []
    for b in range(nb):
        sm = sm_ref[b]
        la_col = pc_ref[0, 1:2, :] * _softplus(sm + pc_ref[0, 0:1, :])
        la_row = pr_ref[0, :, 1:2] * _softplus(smt_ref[b] + pr_ref[0, :, 0:1])
        cum_col, cum_row, _ = _cumsums(la_col, la_row, reverse)
        tot = cum_col[0:1, :] if reverse else cum_col[CHUNK - 1:CHUNK, :]
        per_b.append((cum_col, cum_row, tot, _sigmoid(sm)))

    chains = [(b, h) for b in range(nb) for h in range(GDN_HEADS)]
    for g0 in range(0, len(chains), group):
        grp = chains[g0:g0 + group]

        def lanes(h):
            return slice(h * HEAD_DIM, (h + 1) * HEAD_DIM)

        def l2n(x):
            return x * lax.rsqrt(jnp.sum(x * x, axis=-1, keepdims=True) + 1e-6)

        ca = [8 + d * GDN_HEADS + h for _, h in grp]
        cb = [16 + d * GDN_HEADS + h for _, h in grp]
        qn = [l2n(q_ref[b, :, lanes(h)]) * (HEAD_DIM ** -0.5) for b, h in grp]
        kn = [l2n(k_ref[b, :, lanes(h)]) for b, h in grp]
        vh = [v_ref[b, :, lanes(h)] for b, h in grp]
        cc = [per_b[b][0][:, c:c + 1] for (b, _), c in zip(grp, ca)]
        cr = [per_b[b][1][c:c + 1, :] for (b, _), c in zip(grp, ca)]
        th = [per_b[b][2][:, c:c + 1] for (b, _), c in zip(grp, ca)]
        beta = [per_b[b][3][:, c:c + 1] for (b, _), c in zip(grp, cb)]
        st = [s_s[b * GDN_HEADS + h] for b, h in grp]
        dec = _each(lambda c, r: jnp.where(mask, jnp.exp(c - r), 0.0), cc, cr)
        kb = _each(lambda k, bt: k * bt, kn, beta)
        ms = _each(lambda a, k, dc: jnp.where(strict, _dot_nt(a, k) * dc, 0.0), kb, kn, dec)
        attn = _each(lambda q, k, dc: _dot_nt(q, k) * dc, qn, kn, dec)
        qs = _each(lambda q, c, s: _dot(q * jnp.exp(c), s), qn, cc, st)
        accs = _unit_tri_inverse_minus_eye(ms, blk_i, blk_j)
        rhs = _each(lambda v, bt, k, c: jnp.concatenate([v * bt, k * jnp.exp(c)], axis=1), vh, beta, kb, cc)
        sol = _each(lambda r, a: r + _dot(a, r), rhs, accs)
        v_new = _each(lambda s_, s: s_[:, :HEAD_DIM] - _dot(s_[:, HEAD_DIM:], s), sol, st)
        outs = _each(lambda o, a, v: o + _dot(a, v), qs, attn, v_new)
        new_s = _each(lambda t, s, k, c, v: jnp.exp(t) * s + _dot_tn(k * jnp.exp(t - c), v), th, st, kn, cc, v_new)
        for (b, h), o, s in zip(grp, outs, new_s):
            o_ref[b, :, lanes(h)] = o
            s_s[b * GDN_HEADS + h] = s


def _gdn_call(cv3, p3, smt3, pc, pr, lc, d, reverse):
    nb, seq, _ = cv3.shape
    nc = seq // CHUNK
    nctx = lc // CHUNK

    def ch(i):
        return _chunk_order(i, nctx, nc, reverse)

    def cspec(col):
        return pl.BlockSpec((nb, CHUNK, GW), lambda i: (0, ch(i), col))

    return pl.pallas_call(
        functools.partial(_gdn_body, reverse=reverse, d=d, nb=nb, group=GDN_GROUP),
        out_shape=jax.ShapeDtypeStruct((nb, seq, GW), f32),
        grid=(nc,),
        in_specs=[cspec(0), cspec(1), cspec(2),
                  pl.BlockSpec((nb, CHUNK, 128), lambda i: (0, ch(i), P_SMALL // 128)),
                  pl.BlockSpec((nb, 32, CHUNK), lambda i: (0, 0, ch(i))),
                  pl.BlockSpec((1, 2, 128), lambda i: (d, 0, 0)),
                  pl.BlockSpec((1, 32, 2), lambda i: (d, 0, 0))],
        out_specs=pl.BlockSpec((nb, CHUNK, GW), lambda i: (0, ch(i), 0)),
        scratch_shapes=[pltpu.VMEM((nb * GDN_HEADS, HEAD_DIM, HEAD_DIM), f32)],
        compiler_params=_cparams(("arbitrary",)),
        name="gdn_rev" if reverse else "gdn_fwd",
    )(cv3, cv3, cv3, p3, smt3, pc, pr)


def _rms_lanes(x, w, eps):
    return x * lax.rsqrt(jnp.mean(x * x, axis=-1, keepdims=True) + eps) * w


def _finish_body(hf, hb, rf, rb, sf, sb, gf, gb, gate, rg, cz, dz, xs, retw, ssdd, ssdw, gdnw, o_ref):
    g = gate[...]
    gelu = 0.5 * g * (1.0 + jnp.tanh(math.sqrt(2.0 / math.pi) * (g + 0.044715 * (g * g * g))))
    o_ref[:, 0:GW] = (gelu * (hf[...] + hb[...])).astype(bf16)

    ro = rf[...] + rb[...]
    rgate = _silu(rg[...])
    go = gf[...] + gb[...]
    ggate = _silu(dz[...])
    for h in range(RET_HEADS):
        lanes = slice(h * HEAD_DIM, (h + 1) * HEAD_DIM)
        o_ref[:, GW + h * HEAD_DIM:GW + (h + 1) * HEAD_DIM] = (
            rgate[:, lanes] * _rms_lanes(ro[:, lanes], retw[:, lanes], 1e-6)).astype(bf16)
        o_ref[:, 3 * GW + h * HEAD_DIM:3 * GW + (h + 1) * HEAD_DIM] = (
            _rms_lanes(go[:, lanes], gdnw[:, lanes], 1e-6) * ggate[:, lanes]).astype(bf16)

    sy = (sf[...] + sb[...] + ssdd[...] * xs[...]) * _silu(cz[...])
    o_ref[:, 2 * GW:3 * GW] = _rms_lanes(sy, ssdw[...], 1e-6).astype(bf16)


def _finish_call(scans, p, cv, retw, ssdd, ssdw, gdnw):
    n = p.shape[0]
    tm = ROW_TILE

    def rowspec(col):
        return pl.BlockSpec((tm, GW), lambda i: (i, col))

    vec = pl.BlockSpec((1, GW), lambda i: (0, 0))
    return pl.pallas_call(
        _finish_body,
        out_shape=jax.ShapeDtypeStruct((n, D_MODEL), bf16),
        grid=(n // tm,),
        in_specs=[rowspec(0)] * 8 + [rowspec(P_GATE // GW), rowspec(P_G // GW), rowspec(P_CZ // GW),
                                     rowspec(P_DZ // GW), rowspec(4), vec, vec, vec, vec],
        out_specs=pl.BlockSpec((tm, D_MODEL), lambda i: (i, 0)),
        compiler_params=_cparams(("arbitrary",)),
        name="mixer_finish",
    )(*scans, p, p, p, p, cv, retw, ssdd, ssdw, gdnw)


def _layer_norm_rows(t, w, b):
    mu = jnp.mean(t, axis=-1, keepdims=True)
    tc = t - mu
    var = jnp.mean(tc * tc, axis=-1, keepdims=True)
    return tc * lax.rsqrt(var + 1e-5) * w + b


HALF = D_MODEL // 2
HI_MASK = 0xFFFF0000


def _pack_bf16_pair(lo, hi):
    lo_bits = lax.bitcast_convert_type(lo.astype(bf16).astype(f32), jnp.uint32)
    hi_bits = lax.bitcast_convert_type(hi.astype(bf16).astype(f32), jnp.uint32)
    return hi_bits | (lo_bits >> 16)


def _unpack_bf16_pair(w):
    lo = lax.bitcast_convert_type(w << 16, f32)
    hi = lax.bitcast_convert_type(w & jnp.uint32(HI_MASK), f32)
    return lo, hi


TOKEN_ROWS = HALF // 128


def _store_token_tiles(ref, packed):
    m = packed.shape[0]
    for c in range(TOKEN_ROWS):
        ref[pl.ds(c, m, stride=TOKEN_ROWS), :] = packed[:, c * 128:(c + 1) * 128]


def _load_token_tiles(ref, m):
    return jnp.concatenate([ref[pl.ds(c, m, stride=TOKEN_ROWS), :] for c in range(TOKEN_ROWS)], axis=1)


def _split_bf16(x):
    hi = x.astype(bf16)
    return hi, (x - hi.astype(f32)).astype(bf16)


def _outproj_body(y_ref, w_ref, x_ref, g1_ref, lw_ref, lb_ref, sc_ref, sh_ref, rwh_ref, rwl_ref,
                  x1_ref, u_ref, lg_ref, *, alpha):
    y = jnp.dot(y_ref[...], w_ref[...], preferred_element_type=f32)
    x1 = _layer_norm_rows(alpha * x_ref[...] + g1_ref[0] * y, lw_ref[...], lb_ref[...])
    x1_ref[...] = x1
    u = x1 * (1.0 + sc_ref[0]) + sh_ref[0]
    _store_token_tiles(u_ref, _pack_bf16_pair(u[:, :HALF], u[:, HALF:]))
    uh, ul = _split_bf16(u)
    rwh = rwh_ref[...]
    lg = (jnp.dot(uh, rwh, preferred_element_type=f32) + jnp.dot(ul, rwh, preferred_element_type=f32)
          + jnp.dot(uh, rwl_ref[...], preferred_element_type=f32))
    lg_ref[...] = lg.T[:N_EXPERTS, :]


def _outproj_call(ycat, w_bf, x, g1, lw, lb, sc2, sh2, rw_hi, rw_lo, alpha):
    n, d = x.shape
    tm = ROW_TILE
    mod = pl.BlockSpec((1, 1, d), lambda i: (i, 0, 0))
    vec = pl.BlockSpec((1, d), lambda i: (0, 0))
    rws = pl.BlockSpec((d, 2 * N_EXPERTS), lambda i: (0, 0))
    return pl.pallas_call(
        functools.partial(_outproj_body, alpha=alpha),
        out_shape=(jax.ShapeDtypeStruct((n, d), f32), jax.ShapeDtypeStruct((n * TOKEN_ROWS, 128), jnp.uint32),
                   jax.ShapeDtypeStruct((N_EXPERTS, n), f32)),
        grid=(n // tm,),
        in_specs=[pl.BlockSpec((tm, d), lambda i: (i, 0)),
                  pl.BlockSpec((d, d), lambda i: (0, 0)),
                  pl.BlockSpec((tm, d), lambda i: (i, 0)),
                  mod, vec, vec, mod, mod, rws, rws],
        out_specs=(pl.BlockSpec((tm, d), lambda i: (i, 0)), pl.BlockSpec((tm * TOKEN_ROWS, 128), lambda i: (i, 0)),
                   pl.BlockSpec((N_EXPERTS, tm), lambda i: (0, i))),
        compiler_params=_cparams(("arbitrary",)),
        name="outproj_ln1",
    )(ycat, w_bf, x, g1, lw, lb, sc2, sh2, rw_hi, rw_lo)


EXPERT_TILE = 256


def _num_expert_tiles(n_pairs):
    return (n_pairs + N_EXPERTS * (EXPERT_TILE - 1) + EXPERT_TILE - 1) // EXPERT_TILE


def _route_rank_body(lg_ref, bias_ref, idx_ref, wt_ref, rank_ref, cnt_ref, carry_s):
    i = pl.program_id(0)

    @pl.when(i == 0)
    def _():
        carry_s[...] = jnp.zeros_like(carry_s)

    tn = lg_ref.shape[1]
    scores = _sigmoid(lg_ref[...])
    sel = scores + bias_ref[...]
    gsz = N_EXPERTS // N_EXPERT_GROUPS
    neg = -jnp.inf
    sel3 = sel.reshape(N_EXPERT_GROUPS, gsz, tn)
    io3 = lax.broadcasted_iota(jnp.int32, sel3.shape, 1)
    m1 = jnp.max(sel3, axis=1)
    first = jnp.min(jnp.where(sel3 == m1[:, None, :], io3, gsz), axis=1)
    m2 = jnp.max(jnp.where(io3 == first[:, None, :], neg, sel3), axis=1)
    gscore = m1 + m2
    iog = lax.broadcasted_iota(jnp.int32, gscore.shape, 0)
    gsel = None
    for _ in range(TOPK_GROUPS):
        gm = jnp.max(gscore, axis=0, keepdims=True)
        gi = jnp.min(jnp.where(gscore == gm, iog, N_EXPERT_GROUPS), axis=0, keepdims=True)
        hit = iog == gi
        gsel = hit if gsel is None else jnp.logical_or(gsel, hit)
        gscore = jnp.where(hit, neg, gscore)
    emask = jnp.broadcast_to(gsel[:, None, :], sel3.shape).reshape(N_EXPERTS, tn)
    cand = jnp.where(emask, sel, neg)
    ioe = lax.broadcasted_iota(jnp.int32, cand.shape, 0)
    idxs = []
    wts = []
    hits = []
    for _ in range(TOP_K):
        cm = jnp.max(cand, axis=0, keepdims=True)
        ci = jnp.min(jnp.where(cand == cm, ioe, N_EXPERTS), axis=0, keepdims=True)
        hit = ioe == ci
        idxs.append(ci)
        hits.append(hit)
        wts.append(jnp.sum(jnp.where(hit, scores, 0.0), axis=0, keepdims=True))
        cand = jnp.where(hit, neg, cand)
    wsum = wts[0]
    for w in wts[1:]:
        wsum = wsum + w
    idx_ref[...] = jnp.concatenate(idxs, axis=0)
    wnorm = jnp.concatenate([w / wsum * ROUTED_SCALE for w in wts], axis=0)
    wt_ref[...] = jnp.concatenate([wnorm, jnp.zeros((128 - TOP_K, tn), f32)], axis=0).T

    r_io = lax.broadcasted_iota(jnp.int32, (tn, tn), 0)
    c_io = lax.broadcasted_iota(jnp.int32, (tn, tn), 1)
    prefix = jnp.where(r_io <= c_io, 1.0, 0.0).astype(bf16)
    base = carry_s[...]
    ranks = []
    for hit in hits:
        onehot = jnp.where(hit, 1.0, 0.0)
        cum = jnp.dot(onehot.astype(bf16), prefix, preferred_element_type=f32)
        ranks.append(jnp.sum(onehot * (cum - 1.0 + base), axis=0, keepdims=True))
        base = base + cum[:, tn - 1:tn]
    carry_s[...] = base
    rank_ref[...] = jnp.concatenate(ranks, axis=0).astype(jnp.int32)
    cnt_ref[...] = jnp.broadcast_to(base, cnt_ref.shape)


def _route_rank_call(logits_t, bias):
    n = logits_t.shape[1]
    tn = ROW_TILE
    kt = pl.BlockSpec((TOP_K, tn), lambda i: (0, i))
    return pl.pallas_call(
        _route_rank_body,
        out_shape=(jax.ShapeDtypeStruct((TOP_K, n), jnp.int32),
                   jax.ShapeDtypeStruct((n, 128), f32),
                   jax.ShapeDtypeStruct((TOP_K, n), jnp.int32),
                   jax.ShapeDtypeStruct((N_EXPERTS, 128), f32)),
        grid=(n // tn,),
        in_specs=[pl.BlockSpec((N_EXPERTS, tn), lambda i: (0, i)),
                  pl.BlockSpec((N_EXPERTS, 1), lambda i: (0, 0))],
        out_specs=(kt, pl.BlockSpec((tn, 128), lambda i: (i, 0)), kt,
                   pl.BlockSpec((N_EXPERTS, 128), lambda i: (0, 0))),
        scratch_shapes=[pltpu.VMEM((N_EXPERTS, 1), f32)],
        compiler_params=_cparams(("arbitrary",)),
        name="router_rank",
    )(logits_t, bias)


def _expert_layout(counts, nt):
    tiles_e = (counts + EXPERT_TILE - 1) // EXPERT_TILE
    tend = jnp.cumsum(tiles_e)
    tstart = tend - tiles_e
    n_used = tend[-1:].astype(jnp.int32)
    tile = jnp.arange(nt, dtype=jnp.int32)
    te = jnp.minimum(jnp.sum((tend[None, :] <= tile[:, None]).astype(jnp.int32), axis=1), N_EXPERTS - 1)
    last_tile = jnp.where(tiles_e > 0, tend - 1, -1).astype(jnp.int32)
    pstart = (tstart * EXPERT_TILE).astype(f32)[:, None]
    return te.astype(jnp.int32), n_used, last_tile, pstart


def _dispatch_body(last_ref, nu_ref, u_ref, idx_ref, rank_ref, pstart_ref, pos_ref, xs_hbm,
                   zbuf, pos_s, zsem, psem, ssem):
    i = pl.program_id(0)
    tn = idx_ref.shape[1]
    tr = TOKEN_ROWS
    tile_rows = EXPERT_TILE * tr
    n_tiles = xs_hbm.shape[0] // tile_rows

    @pl.when(i == 0)
    def _():
        zbuf[...] = jnp.zeros_like(zbuf)

        def zero_tile(t):
            row = pl.multiple_of(t * tile_rows, tile_rows)
            return pltpu.make_async_copy(zbuf, xs_hbm.at[pl.ds(row, tile_rows)], zsem)

        for e in range(N_EXPERTS):
            @pl.when(last_ref[e] >= 0)
            def _(e=e):
                zero_tile(last_ref[e]).start()
        lax.fori_loop(nu_ref[0], n_tiles, lambda t, c: (zero_tile(t).start(), c)[1], 0)
        for e in range(N_EXPERTS):
            @pl.when(last_ref[e] >= 0)
            def _():
                zero_tile(0).wait()
        lax.fori_loop(nu_ref[0], n_tiles, lambda t, c: (zero_tile(0).wait(), c)[1], 0)

    ioe = lax.broadcasted_iota(jnp.int32, (N_EXPERTS, tn), 0)
    pstart = pstart_ref[...]
    pos = []
    for k in range(TOP_K):
        start_k = jnp.sum(jnp.where(ioe == idx_ref[k:k + 1, :], pstart, 0.0), axis=0, keepdims=True)
        pos.append(start_k.astype(jnp.int32) + rank_ref[k:k + 1, :])
    pos_ref[...] = jnp.concatenate(pos, axis=0)
    to_smem = pltpu.make_async_copy(pos_ref, pos_s, psem)
    to_smem.start()
    to_smem.wait()

    def per_token(t, carry):
        src = u_ref.at[pl.ds(pl.multiple_of(t * tr, tr), tr)]
        for k in range(TOP_K):
            row = pl.multiple_of(pos_s[k, t] * tr, tr)
            pltpu.make_async_copy(src, xs_hbm.at[pl.ds(row, tr)], ssem).start(priority=k % 2)
        return carry

    lax.fori_loop(0, tn, per_token, 0, unroll=4)
    for k in range(TOP_K):
        pltpu.make_async_copy(u_ref, xs_hbm.at[pl.ds(0, tn * tr)], ssem).wait()


def _dispatch_call(last_tile, n_used, u_pk, idx, rank, pstart, nt):
    n = idx.shape[1]
    tn = ROW_TILE
    tr = TOKEN_ROWS
    kt = pl.BlockSpec((TOP_K, tn), lambda i, last, nu: (0, i))
    grid_spec = pltpu.PrefetchScalarGridSpec(
        num_scalar_prefetch=2,
        grid=(n // tn,),
        in_specs=[pl.BlockSpec((tn * tr, 128), lambda i, last, nu: (i, 0)), kt, kt,
                  pl.BlockSpec((N_EXPERTS, 1), lambda i, last, nu: (0, 0))],
        out_specs=(kt, pl.BlockSpec(memory_space=pl.ANY)),
        scratch_shapes=[pltpu.VMEM((EXPERT_TILE * tr, 128), jnp.uint32),
                        pltpu.SMEM((TOP_K, tn), jnp.int32),
                        pltpu.SemaphoreType.DMA(()), pltpu.SemaphoreType.DMA(()), pltpu.SemaphoreType.DMA(())])
    return pl.pallas_call(
        _dispatch_body,
        out_shape=(jax.ShapeDtypeStruct((TOP_K, n), jnp.int32),
                   jax.ShapeDtypeStruct((nt * EXPERT_TILE * tr, 128), jnp.uint32)),
        grid_spec=grid_spec,
        compiler_params=_cparams(("arbitrary",)),
        name="dispatch",
    )(last_tile, n_used, u_pk, idx, rank, pstart)


def _sorted_experts_body(te_ref, nu_ref, x_ref, w1_ref, w3_ref, w2_ref, o_ref, w1_s, w3_s, w2_s):
    i = pl.program_id(0)
    nu = nu_ref[0]
    tm = EXPERT_TILE

    @pl.when(i < nu)
    def _():
        prev = te_ref[jnp.maximum(i - 1, 0)]

        @pl.when(jnp.logical_or(i == 0, te_ref[i] != prev))
        def _():
            w1_s[...] = w1_ref[0, 0].astype(bf16)
            w3_s[...] = w3_ref[0, 0].astype(bf16)
            w2_s[...] = w2_ref[0, 0].astype(bf16)

        lo, hi = _unpack_bf16_pair(_load_token_tiles(x_ref, tm))
        lo = lo.astype(bf16)
        hi = hi.astype(bf16)
        h1 = (jnp.dot(lo, w1_s[:HALF, :], preferred_element_type=f32)
              + jnp.dot(hi, w1_s[HALF:, :], preferred_element_type=f32))
        h3 = (jnp.dot(lo, w3_s[:HALF, :], preferred_element_type=f32)
              + jnp.dot(hi, w3_s[HALF:, :], preferred_element_type=f32))
        y = jnp.dot((_silu(h1) * h3).astype(bf16), w2_s[...], preferred_element_type=f32)
        _store_token_tiles(o_ref, _pack_bf16_pair(y[:, :HALF], y[:, HALF:]))

    @pl.when(i >= nu)
    def _():
        o_ref[...] = jnp.zeros_like(o_ref)


def _sorted_experts_call(te, n_used, xs, w1, w3, w2, layer):
    nt = te.shape[0]
    _, ne, d, fe = w1.shape
    tm = EXPERT_TILE
    rows = tm * TOKEN_ROWS

    def wmap(i, te_r, nu_r):
        return (layer, te_r[jnp.minimum(i, nu_r[0] - 1)], 0, 0)

    grid_spec = pltpu.PrefetchScalarGridSpec(
        num_scalar_prefetch=2,
        grid=(nt,),
        in_specs=[pl.BlockSpec((rows, 128), lambda i, te_r, nu_r: (jnp.minimum(i, nu_r[0] - 1), 0)),
                  pl.BlockSpec((1, 1, d, fe), wmap), pl.BlockSpec((1, 1, d, fe), wmap),
                  pl.BlockSpec((1, 1, fe, d), wmap)],
        out_specs=pl.BlockSpec((rows, 128), lambda i, te_r, nu_r: (i, 0)),
        scratch_shapes=[pltpu.VMEM((d, fe), bf16), pltpu.VMEM((d, fe), bf16), pltpu.VMEM((fe, d), bf16)])
    return pl.pallas_call(
        _sorted_experts_body,
        out_shape=jax.ShapeDtypeStruct(xs.shape, jnp.uint32),
        grid_spec=grid_spec,
        compiler_params=_cparams(("arbitrary",)),
        name="experts_sorted",
    )(te, n_used, xs, w1, w3, w2)


def _gather_combine_body(pos_ref, wt_ref, u_ref, x_ref, g2_ref, lw_ref, lb_ref, s1_ref, s3_ref, s2_ref, ys_hbm,
                         o_ref, pos_s, gbuf, psem, gsem, *, alpha):
    tn = x_ref.shape[0]
    tr = TOKEN_ROWS
    to_smem = pltpu.make_async_copy(pos_ref, pos_s, psem)
    to_smem.start()
    to_smem.wait()

    def per_token(t, carry):
        dst_row = pl.multiple_of(t * tr, tr)
        for k in range(TOP_K):
            row = pl.multiple_of(pos_s[k, t] * tr, tr)
            pltpu.make_async_copy(ys_hbm.at[pl.ds(row, tr)], gbuf.at[k, pl.ds(dst_row, tr)],
                                  gsem).start(priority=k % 2)
        return carry

    lax.fori_loop(0, tn, per_token, 0, unroll=4)

    ulo, uhi = _unpack_bf16_pair(_load_token_tiles(u_ref, tn))
    ulo = ulo.astype(bf16)
    uhi = uhi.astype(bf16)
    a1 = (jnp.dot(ulo, s1_ref[:HALF, :], preferred_element_type=f32)
          + jnp.dot(uhi, s1_ref[HALF:, :], preferred_element_type=f32))
    a3 = (jnp.dot(ulo, s3_ref[:HALF, :], preferred_element_type=f32)
          + jnp.dot(uhi, s3_ref[HALF:, :], preferred_element_type=f32))
    shared = jnp.dot((_silu(a1) * a3).astype(bf16), s2_ref[...], preferred_element_type=f32)

    for k in range(TOP_K):
        pltpu.make_async_copy(ys_hbm.at[pl.ds(0, tn * tr)], gbuf.at[k], gsem).wait()
    wt = wt_ref[...]
    lo = None
    hi = None
    for k in range(TOP_K):
        lo_k, hi_k = _unpack_bf16_pair(_load_token_tiles(gbuf.at[k], tn))
        gate = wt[:, k:k + 1]
        lo = gate * lo_k if lo is None else lo + gate * lo_k
        hi = gate * hi_k if hi is None else hi + gate * hi_k
    f = jnp.concatenate([lo, hi], axis=1) + shared
    o_ref[...] = _layer_norm_rows(alpha * x_ref[...] + g2_ref[0] * f, lw_ref[...], lb_ref[...])


def _gather_combine_call(pos, wt_tm, u_pk, x1, g2, lw, lb, s1, s3, s2, ys, alpha):
    n, d = x1.shape
    tn = ROW_TILE
    tr = TOKEN_ROWS
    fe = s1.shape[1]
    row = pl.BlockSpec((tn, d), lambda i: (i, 0))
    vec = pl.BlockSpec((1, d), lambda i: (0, 0))
    return pl.pallas_call(
        functools.partial(_gather_combine_body, alpha=alpha),
        out_shape=jax.ShapeDtypeStruct((n, d), f32),
        grid=(n // tn,),
        in_specs=[pl.BlockSpec((TOP_K, tn), lambda i: (0, i)),
                  pl.BlockSpec((tn, 128), lambda i: (i, 0)),
                  pl.BlockSpec((tn * tr, 128), lambda i: (i, 0)), row,
                  pl.BlockSpec((1, 1, d), lambda i: (i, 0, 0)), vec, vec,
                  pl.BlockSpec((d, fe), lambda i: (0, 0)), pl.BlockSpec((d, fe), lambda i: (0, 0)),
                  pl.BlockSpec((fe, d), lambda i: (0, 0)),
                  pl.BlockSpec(memory_space=pl.ANY)],
        out_specs=row,
        scratch_shapes=[pltpu.SMEM((TOP_K, tn), jnp.int32),
                        pltpu.VMEM((TOP_K, tn * tr, 128), jnp.uint32),
                        pltpu.SemaphoreType.DMA(()), pltpu.SemaphoreType.DMA(())],
        compiler_params=_cparams(("arbitrary",)),
        name="gather_combine_ln2",
    )(pos, wt_tm, u_pk, x1, g2, lw, lb, s1, s3, s2, ys)


def _block_diag(w):
    g, n, _ = w.shape
    eye = jnp.eye(g, dtype=w.dtype)
    return (eye[:, None, :, None] * w[:, :, None, :]).reshape(g * n, g * n)


def _rope_tables(seq, lc):
    half = HEAD_DIM // 2
    t = jnp.arange(seq - lc, dtype=f32)
    inv = ROPE_BASE ** (-jnp.arange(0, half, 2, dtype=f32) / half)
    ar = jnp.floor(t / GRID_W)[:, None] * inv[None, :]
    ac = (t - jnp.floor(t / GRID_W) * GRID_W)[:, None] * inv[None, :]
    cos = jnp.concatenate([jnp.cos(ar), jnp.cos(ar), jnp.cos(ac), jnp.cos(ac)], axis=1)
    sin = jnp.concatenate([-jnp.sin(ar), jnp.sin(ar), -jnp.sin(ac), jnp.sin(ac)], axis=1)
    cos = jnp.concatenate([jnp.ones((lc, HEAD_DIM), f32), cos], axis=0)
    sin = jnp.concatenate([jnp.zeros((lc, HEAD_DIM), f32), sin], axis=0)
    return cos, sin


def _pad_lanes(v, offset, width=128):
    return jnp.zeros((width,), f32).at[offset:offset + v.shape[0]].set(v)


def kernel(x, c, ctx, c_ctx, w_ada, b_ada, w_in, lru_conv_w, lru_conv_b, lru_gate_a_w, lru_gate_a_b,
           lru_gate_x_w, lru_gate_x_b, lru_lambda, ret_log_decay, ret_norm_w, ssd_conv_w, ssd_conv_b,
           ssd_dt_bias, ssd_a_log, ssd_d, ssd_norm_w, gdn_conv_w, gdn_dt_bias, gdn_a_log, gdn_norm_w,
           w_out, ln1_w, ln1_b, router_w, router_bias, exp_w1, exp_w3, exp_w2, sh_w1, sh_w3, sh_w2,
           ln2_w, ln2_b):
    nb, lat, d = x.shape
    lc = ctx.shape[1]
    depth = w_ada.shape[0]
    seq = lc + lat
    n = nb * seq
    ntile = n // ROW_TILE
    tiles_per_seq = seq // ROW_TILE
    alpha = (2 * depth) ** 0.25
    assert d == D_MODEL and lc == ROW_TILE and lat % ROW_TILE == 0 and nb <= 7

    xall = jnp.concatenate([ctx, x], axis=1).reshape(n, d)

    svec = jnp.zeros((8, d), f32).at[0].set(c_ctx).at[1:1 + nb].set(c)
    modtab = _ada_call(svec, w_ada, b_ada).reshape(depth, 8, 6, d)
    tile = jnp.arange(ntile)
    tile_row = jnp.where(tile % tiles_per_seq == 0, 0, 1 + tile // tiles_per_seq)
    cos, sin = _rope_tables(seq, lc)

    for l in range(depth):
        mods = [modtab[l, :, k, :][tile_row][:, None, :] for k in range(6)]
        sh1, sc1, g1, sh2, sc2, g2 = mods

        wl = w_in[l]
        w_re = jnp.concatenate([wl[:, :3584], wl[:, 5896:6408], wl[:, 4360:5896], wl[:, 3584:4352],
                                wl[:, 4352:4360], wl[:, 6408:6424],
                                jnp.zeros((d, IN_COLS_PAD - 6424), f32)], axis=1).astype(bf16)
        p = _inproj_call(xall, sc1, sh1, w_re)

        wcat = jnp.concatenate([gdn_conv_w[l], lru_conv_w[l], ssd_conv_w[l]], axis=1)
        bcat = jnp.concatenate([jnp.zeros((3 * GW,), f32), lru_conv_b[l], ssd_conv_b[l]])[None, :]
        cv = _conv_call(p, wcat, bcat, nb, seq, lc)
        p3 = p.reshape(nb, seq, IN_COLS_PAD)
        cv3 = cv.reshape(nb, seq, CV_COLS)
        smt3 = jnp.transpose(p3[:, :, P_SMALL:P_SMALL + 32], (0, 2, 1))

        wg = jnp.stack([jnp.concatenate([_block_diag(lru_gate_a_w[l, dd]), _block_diag(lru_gate_x_w[l, dd])],
                                        axis=1) for dd in range(2)]).astype(bf16)
        bg = jnp.concatenate([lru_gate_a_b[l], lru_gate_x_b[l]], axis=1)[:, None, :]
        sp = jax.nn.softplus(-lru_lambda[l])[:, None, :]
        hs = [_lru_call(cv3, wg, bg, sp, lc, dd, bool(dd)).reshape(n, GW) for dd in range(2)]

        lg = -jnp.exp(ret_log_decay[l].astype(f32))
        rs = [_ret_call(p3, lg[dd], cos, sin, lc, bool(dd)).reshape(n, GW) for dd in range(2)]

        a_ssd = -jnp.exp(ssd_a_log[l])
        pc = jnp.stack([jnp.stack([_pad_lanes(ssd_dt_bias[l, dd], 0), _pad_lanes(a_ssd[dd], 0)])
                        for dd in range(2)])
        pr = jnp.stack([jnp.stack([ssd_dt_bias[l, dd], a_ssd[dd]], axis=1) for dd in range(2)])
        ss = [_ssd_call(cv3, p3, smt3, pc, pr, lc, dd, bool(dd)).reshape(n, GW) for dd in range(2)]

        a_gdn = -jnp.exp(gdn_a_log[l])
        pcg = jnp.stack([jnp.stack([_pad_lanes(gdn_dt_bias[l, dd], 8 + 4 * dd),
                                    _pad_lanes(a_gdn[dd], 8 + 4 * dd)]) for dd in range(2)])
        prg = jnp.stack([jnp.stack([_pad_lanes(gdn_dt_bias[l, dd], 8 + 4 * dd, 32),
                                    _pad_lanes(a_gdn[dd], 8 + 4 * dd, 32)], axis=1) for dd in range(2)])
        gs = [_gdn_call(cv3, p3, smt3, pcg, prg, lc, dd, bool(dd)).reshape(n, GW) for dd in range(2)]

        ycat = _finish_call(hs + rs + ss + gs, p, cv, ret_norm_w[l][None, :],
                            jnp.repeat(ssd_d[l], SSD_HEAD_DIM)[None, :], ssd_norm_w[l][None, :],
                            jnp.tile(gdn_norm_w[l], GDN_HEADS)[None, :])

        rw = jnp.concatenate([router_w[l], jnp.zeros((d, N_EXPERTS), f32)], axis=1)
        rw_hi = rw.astype(bf16)
        rw_lo = (rw - rw_hi.astype(f32)).astype(bf16)
        x1, u_pk, logits_t = _outproj_call(ycat, w_out[l].astype(bf16), xall, g1, ln1_w[l][None, :],
                                           ln1_b[l][None, :], sc2, sh2, rw_hi, rw_lo, alpha)
        idx, wt_tm, rank, cnt = _route_rank_call(logits_t, router_bias[l][:, None])
        nt = _num_expert_tiles(TOP_K * n)
        te, n_used, last_tile, pstart = _expert_layout(cnt[:, 0].astype(jnp.int32), nt)
        pos, xs = _dispatch_call(last_tile, n_used, u_pk, idx, rank, pstart, nt)
        ys = _sorted_experts_call(te, n_used, xs, exp_w1, exp_w3, exp_w2, l)
        xall = _gather_combine_call(pos, wt_tm, u_pk, x1, g2, ln2_w[l][None, :], ln2_b[l][None, :],
                                    sh_w1[l].astype(bf16), sh_w3[l].astype(bf16), sh_w2[l].astype(bf16),
                                    ys, alpha)

    return xall.reshape(nb, seq, d)[:, lc:, :]
```

```python
import functools
import math

import jax
import jax.numpy as jnp
from jax import lax
from jax.experimental import pallas as pl
from jax.experimental.pallas import tpu as pltpu

f32 = jnp.float32
bf16 = jnp.bfloat16
HIGHEST = lax.Precision.HIGHEST

D_MODEL = 2048
GW = 512
GRID_W = 64
CONV_K = 4
LRU_C = 8.0
RET_HEADS = 4
HEAD_DIM = 128
ROPE_BASE = 10000.0
SSD_HEADS = 8
SSD_HEAD_DIM = 64
SSD_STATE = 64
GDN_HEADS = 4
N_EXPERTS = 64
TOP_K = 8
N_EXPERT_GROUPS = 8
TOPK_GROUPS = 4
D_EXPERT = 256
ROUTED_SCALE = 2.5

ROW_TILE = 256
CHUNK = 128
IN_COLS_PAD = 6528
CV_COLS = 2816
VMEM_LIMIT = 56 * 1024 * 1024

P_XB, P_GATE, P_Q, P_K, P_V, P_G, P_CZ, P_DZ, P_DQKV, P_XBC, P_SMALL = (
    0, 512, 1024, 1536, 2048, 2560, 3072, 3584, 4096, 5632, 6400)


def _cparams(sem):
    return pltpu.CompilerParams(dimension_semantics=sem, vmem_limit_bytes=VMEM_LIMIT)


def _sigmoid(x):
    return 1.0 / (1.0 + jnp.exp(-x))


def _silu(x):
    return x * _sigmoid(x)


def _softplus(x):
    return jnp.maximum(x, 0.0) + jnp.log(1.0 + jnp.exp(-jnp.abs(x)))


def _dot(a, b):
    return jnp.dot(a.astype(bf16), b.astype(bf16), preferred_element_type=f32)


def _dot_nt(a, b):
    return lax.dot_general(a.astype(bf16), b.astype(bf16), (((1,), (1,)), ((), ())),
                           preferred_element_type=f32)


def _dot_tn(a, b):
    return lax.dot_general(a.astype(bf16), b.astype(bf16), (((0,), (0,)), ((), ())),
                           preferred_element_type=f32)


def _chunk_order(i, n_ctx, n_all, reverse):
    if not reverse:
        return i
    return jnp.where(i < n_ctx, n_ctx - 1 - i, n_all + n_ctx - 1 - i)


def _ada_body(s_ref, w_ref, b_ref, o_ref):
    xh, xl = _split_bf16(_silu(s_ref[...]))
    wh, wl = _split_bf16(w_ref[0])
    acc = jnp.dot(xh, wh, preferred_element_type=f32) + jnp.dot(xl, wh, preferred_element_type=f32)
    o_ref[0] = acc + jnp.dot(xh, wl, preferred_element_type=f32) + b_ref[0]


def _ada_call(svec, w_ada, b_ada):
    nl, d, d6 = w_ada.shape
    tn = 1024
    return pl.pallas_call(
        _ada_body,
        out_shape=jax.ShapeDtypeStruct((nl, 8, d6), f32),
        grid=(nl, d6 // tn),
        in_specs=[pl.BlockSpec((8, d), lambda l, j: (0, 0)),
                  pl.BlockSpec((1, d, tn), lambda l, j: (l, 0, j)),
                  pl.BlockSpec((1, 1, tn), lambda l, j: (l, 0, j))],
        out_specs=pl.BlockSpec((1, 8, tn), lambda l, j: (l, 0, j)),
        compiler_params=_cparams(("arbitrary", "arbitrary")),
        name="ada",
    )(svec, w_ada, b_ada.reshape(nl, 1, d6))


def _inproj_body(x_ref, sc_ref, sh_ref, w_ref, o_ref, *, sub):
    for s in range(sub):
        rows = slice(s * ROW_TILE, (s + 1) * ROW_TILE)
        xm = x_ref[rows, :] * (1.0 + sc_ref[s]) + sh_ref[s]
        o_ref[rows, :] = _dot(xm, w_ref[...])


def _inproj_call(x, sc, sh, w_bf):
    n, d = x.shape
    sub = 2
    tm = sub * ROW_TILE
    tn = IN_COLS_PAD // 3
    return pl.pallas_call(
        functools.partial(_inproj_body, sub=sub),
        out_shape=jax.ShapeDtypeStruct((n, IN_COLS_PAD), f32),
        grid=(3, n // tm),
        in_specs=[pl.BlockSpec((tm, d), lambda j, i: (i, 0)),
                  pl.BlockSpec((sub, 1, d), lambda j, i: (i, 0, 0)),
                  pl.BlockSpec((sub, 1, d), lambda j, i: (i, 0, 0)),
                  pl.BlockSpec((d, tn), lambda j, i: (0, j))],
        out_specs=pl.BlockSpec((tm, tn), lambda j, i: (i, j)),
        compiler_params=_cparams(("arbitrary", "arbitrary")),
        name="inproj",
    )(x, sc, sh, w_bf)


def _conv_body(x_ref, w_ref, b_ref, o_ref, *, seq, lc):
    j = pl.program_id(1)
    w = w_ref[...]
    bias = b_ref[...]
    use_act = jnp.logical_or(j < 6, j >= 8)
    row = lax.broadcasted_iota(jnp.int32, (CHUNK, 1), 0)

    def chunk(c, carry):
        r0 = pl.multiple_of(c * CHUNK, CHUNK)
        cur = x_ref[pl.ds(r0, CHUNK), :]
        prev = x_ref[pl.ds(pl.multiple_of(jnp.maximum(r0 - 8, 0), 8), 8), :]
        nxt = x_ref[pl.ds(pl.multiple_of(jnp.minimum(r0 + CHUNK, seq - 8), 8), 8), :]
        ext = jnp.concatenate([prev, cur, nxt], axis=0)
        t = r0 + row
        s0 = jnp.where(r0 < lc, 0, lc)
        s1 = jnp.where(r0 < lc, lc, seq)
        acc = jnp.zeros_like(cur) + bias
        for k in range(CONV_K):
            o = k - CONV_K // 2
            seg = ext[8 + o:8 + o + CHUNK]
            valid = jnp.logical_and(t + o >= s0, t + o < s1)
            acc = acc + jnp.where(valid, seg, 0.0) * w[k:k + 1, :]
        o_ref[pl.ds(r0, CHUNK), :] = jnp.where(use_act, _silu(acc), acc)
        return carry

    lax.fori_loop(0, seq // CHUNK, chunk, 0)


def _conv_call(p, wcat, bcat, nb, seq, lc):
    n = p.shape[0]
    cb = 256

    def in_map(b, j):
        return (b, jnp.where(j < 6, P_DQKV // cb + j, jnp.where(j < 8, j - 6, P_XBC // cb - 8 + j)))

    return pl.pallas_call(
        functools.partial(_conv_body, seq=seq, lc=lc),
        out_shape=jax.ShapeDtypeStruct((n, CV_COLS), f32),
        grid=(nb, CV_COLS // cb),
        in_specs=[pl.BlockSpec((seq, cb), in_map),
                  pl.BlockSpec((CONV_K, cb), lambda b, j: (0, j)),
                  pl.BlockSpec((1, cb), lambda b, j: (0, j))],
        out_specs=pl.BlockSpec((seq, cb), lambda b, j: (b, j)),
        compiler_params=_cparams(("arbitrary", "arbitrary")),
        name="dwconv",
    )(p, wcat, bcat)


def _lru_body(xc_ref, wg_ref, bg_ref, sp_ref, o_ref, a_s, b_s, h_s, *, reverse, nb):
    i = pl.program_id(0)

    @pl.when(i == 0)
    def _():
        h_s[...] = jnp.zeros_like(h_s)

    for b in range(nb):
        xc = xc_ref[b]
        gates = _dot(xc, wg_ref[0]) + bg_ref[0]
        r = _sigmoid(gates[:, :GW])
        ig = _sigmoid(gates[:, GW:])
        a = jnp.exp(-LRU_C * r * sp_ref[0])
        a_s[b] = a
        b_s[b] = jnp.sqrt(1.0 - a * a) * (ig * xc)

    def step(jj, hs):
        t = (ROW_TILE - 1 - jj) if reverse else jj
        out = []
        for b in range(nb):
            h = a_s[b, pl.ds(t, 1), :] * hs[b] + b_s[b, pl.ds(t, 1), :]
            o_ref[b, pl.ds(t, 1), :] = h
            out.append(h)
        return tuple(out)

    hs = lax.fori_loop(0, ROW_TILE, step, tuple(h_s[b] for b in range(nb)), unroll=8)
    for b in range(nb):
        h_s[b] = hs[b]


def _lru_call(cv3, wg, bg, sp, lc, d, reverse):
    nb, seq, _ = cv3.shape
    nt = seq // ROW_TILE
    nctx = lc // ROW_TILE

    def ch(i):
        return _chunk_order(i, nctx, nt, reverse)

    return pl.pallas_call(
        functools.partial(_lru_body, reverse=reverse, nb=nb),
        out_shape=jax.ShapeDtypeStruct((nb, seq, GW), f32),
        grid=(nt,),
        in_specs=[pl.BlockSpec((nb, ROW_TILE, GW), lambda i: (0, ch(i), 3)),
                  pl.BlockSpec((1, GW, 2 * GW), lambda i: (d, 0, 0)),
                  pl.BlockSpec((1, 1, 2 * GW), lambda i: (d, 0, 0)),
                  pl.BlockSpec((1, 1, GW), lambda i: (d, 0, 0))],
        out_specs=pl.BlockSpec((nb, ROW_TILE, GW), lambda i: (0, ch(i), 0)),
        scratch_shapes=[pltpu.VMEM((nb, ROW_TILE, GW), f32), pltpu.VMEM((nb, ROW_TILE, GW), f32),
                        pltpu.VMEM((nb, 1, GW), f32)],
        compiler_params=_cparams(("arbitrary",)),
        name="lru_rev" if reverse else "lru_fwd",
    )(cv3, wg, bg, sp)


def _tri_masks(reverse):
    ii = lax.broadcasted_iota(jnp.int32, (CHUNK, CHUNK), 0)
    jj = lax.broadcasted_iota(jnp.int32, (CHUNK, CHUNK), 1)
    if reverse:
        return jj >= ii, jj > ii, ii, jj
    return ii >= jj, ii > jj, ii, jj


def _rope(x, cos, sin):
    lane = lax.broadcasted_iota(jnp.int32, x.shape, 1)
    swapped = jnp.where(lane % 64 < 32, pltpu.roll(x, 96, 1), pltpu.roll(x, 32, 1))
    return x * cos + swapped * sin


def _each(fn, *lists):
    return [fn(*args) for args in zip(*lists)]


def _ret_body(lg_ref, q_ref, k_ref, v_ref, cos_ref, sin_ref, o_ref, s_s, *, reverse, nb):
    i = pl.program_id(0)

    @pl.when(i == 0)
    def _():
        s_s[...] = jnp.zeros_like(s_s)

    mask, _, ii, jj = _tri_masks(reverse)
    diff = ((jj - ii) if reverse else (ii - jj)).astype(f32)
    col = lax.broadcasted_iota(jnp.int32, (CHUNK, 1), 0).astype(f32)
    cos = cos_ref[...]
    sin = sin_ref[...]
    dec, eq, ek, ec = [], [], [], []
    for h in range(RET_HEADS):
        lg = lg_ref[h]
        dec.append(jnp.where(mask, jnp.exp(lg * diff), 0.0))
        eq.append(jnp.exp(lg * ((CHUNK - col) if reverse else (col + 1.0))))
        ek.append(jnp.exp(lg * (col if reverse else (CHUNK - 1.0 - col))))
        ec.append(jnp.exp(lg * CHUNK))

    def lanes(h):
        return slice(h * HEAD_DIM, (h + 1) * HEAD_DIM)

    chains = [(b, h) for b in range(nb) for h in range(RET_HEADS)]
    hd = [h for _, h in chains]
    qh = [_rope(q_ref[b, :, lanes(h)], cos, sin) * (HEAD_DIM ** -0.5) for b, h in chains]
    kh = [_rope(k_ref[b, :, lanes(h)], cos, sin) for b, h in chains]
    vh = [v_ref[b, :, lanes(h)] for b, h in chains]
    st = [s_s[b * RET_HEADS + h] for b, h in chains]
    sc = _each(lambda q, k, h: _dot_nt(q, k) * dec[h], qh, kh, hd)
    inter = _each(lambda q, s, h: _dot(q * eq[h], s), qh, st, hd)
    ys = _each(lambda s, v, y0: y0 + _dot(s, v), sc, vh, inter)
    new_s = _each(lambda s, k, v, h: ec[h] * s + _dot_tn(k * ek[h], v), st, kh, vh, hd)
    for (b, h), y, s in zip(chains, ys, new_s):
        o_ref[b, :, lanes(h)] = y
        s_s[b * RET_HEADS + h] = s


def _ret_call(p3, lg, cos, sin, lc, reverse):
    nb, seq, _ = p3.shape
    nc = seq // CHUNK
    nctx = lc // CHUNK

    def ch(i):
        return _chunk_order(i, nctx, nc, reverse)

    def pspec(col):
        return pl.BlockSpec((nb, CHUNK, GW), lambda i: (0, ch(i), col // GW))

    return pl.pallas_call(
        functools.partial(_ret_body, reverse=reverse, nb=nb),
        out_shape=jax.ShapeDtypeStruct((nb, seq, GW), f32),
        grid=(nc,),
        in_specs=[pl.BlockSpec(memory_space=pltpu.SMEM),
                  pspec(P_Q), pspec(P_K), pspec(P_V),
                  pl.BlockSpec((CHUNK, HEAD_DIM), lambda i: (ch(i), 0)),
                  pl.BlockSpec((CHUNK, HEAD_DIM), lambda i: (ch(i), 0))],
        out_specs=pl.BlockSpec((nb, CHUNK, GW), lambda i: (0, ch(i), 0)),
        scratch_shapes=[pltpu.VMEM((nb * RET_HEADS, HEAD_DIM, HEAD_DIM), f32)],
        compiler_params=_cparams(("arbitrary",)),
        name="ret_rev" if reverse else "ret_fwd",
    )(lg, p3, p3, p3, cos, sin)


def _cumsums(la_col, la_row, reverse):
    mask, _, ii, jj = _tri_masks(reverse)
    m_col = jnp.where(mask, 1.0, 0.0)
    m_row = jnp.where(jnp.logical_not(mask) | (ii == jj), 1.0, 0.0)
    cum_col = jnp.dot(m_col, la_col, precision=HIGHEST, preferred_element_type=f32)
    cum_row = jnp.dot(la_row, m_row, precision=HIGHEST, preferred_element_type=f32)
    return cum_col, cum_row, mask


def _ssd_body(xs_ref, bc_ref, sm_ref, smt_ref, pc_ref, pr_ref, o_ref, s_s, *, reverse, nb):
    i = pl.program_id(0)

    @pl.when(i == 0)
    def _():
        s_s[...] = jnp.zeros_like(s_s)

    lane = lax.broadcasted_iota(jnp.int32, (1, 128), 1)
    mask, _, _, _ = _tri_masks(reverse)
    prep = []
    for b in range(nb):
        dt_col = _softplus(sm_ref[b] + pc_ref[0, 0:1, :])
        la_col = pc_ref[0, 1:2, :] * dt_col
        la_row = pr_ref[0, :, 1:2] * _softplus(smt_ref[b, 0:SSD_HEADS, :] + pr_ref[0, :, 0:1])
        cum_col, cum_row, _ = _cumsums(la_col, la_row, reverse)
        tot = cum_col[0:1, :] if reverse else cum_col[CHUNK - 1:CHUNK, :]
        bm = bc_ref[b, :, :128]
        cm = bc_ref[b, :, 128:]
        grp = []
        for g in range(2):
            gmask = (lane // SSD_STATE) == g
            grp.append((jnp.where(gmask, cm, 0.0), jnp.where(gmask, bm, 0.0)))
        prep.append((dt_col, cum_col, cum_row, tot, grp))

    scores = {(b, g): _dot_nt(prep[b][4][g][0], prep[b][4][g][1]) for b in range(nb) for g in range(2)}
    heads = [(b, h) for b in range(nb) for h in range(SSD_HEADS)]

    def head_inputs(b, h):
        dt_col, cum_col, cum_row, tot, grp = prep[b]
        pp, hh, g = h // 2, h % 2, h // (SSD_HEADS // 2)
        hmask = (lane // SSD_HEAD_DIM) == hh
        cc = cum_col[:, h:h + 1]
        dec = jnp.where(mask, jnp.exp(cc - cum_row[h:h + 1, :]), 0.0)
        vh = jnp.where(hmask, xs_ref[b, :, pp * 128:(pp + 1) * 128] * dt_col[:, h:h + 1], 0.0)
        sth = jnp.where(hmask, s_s[b * 4 + pp], 0.0)
        return cc, dec, vh, sth, tot[:, h:h + 1], grp[g][0], grp[g][1], scores[(b, g)]

    ins = [head_inputs(b, h) for b, h in heads]
    intra = [_dot(sc * dec, vh) for _, dec, vh, _, _, _, _, sc in ins]
    inter = [_dot(cmg * jnp.exp(cc), sth) for cc, _, _, sth, _, cmg, _, _ in ins]
    upd = [jnp.exp(th) * sth + _dot_tn(bmg * jnp.exp(th - cc), vh) for cc, _, vh, sth, th, _, bmg, _ in ins]
    for j in range(0, len(heads), 2):
        b, h = heads[j]
        pp = h // 2
        o_ref[b, :, pp * 128:(pp + 1) * 128] = (intra[j] + inter[j]) + (intra[j + 1] + inter[j + 1])
        s_s[b * 4 + pp] = upd[j] + upd[j + 1]


def _ssd_call(cv3, p3, smt3, pc, pr, lc, d, reverse):
    nb, seq, _ = cv3.shape
    nc = seq // CHUNK
    nctx = lc // CHUNK

    def ch(i):
        return _chunk_order(i, nctx, nc, reverse)

    return pl.pallas_call(
        functools.partial(_ssd_body, reverse=reverse, nb=nb),
        out_shape=jax.ShapeDtypeStruct((nb, seq, GW), f32),
        grid=(nc,),
        in_specs=[pl.BlockSpec((nb, CHUNK, GW), lambda i: (0, ch(i), 4)),
                  pl.BlockSpec((nb, CHUNK, 256), lambda i: (0, ch(i), 10)),
                  pl.BlockSpec((nb, CHUNK, 128), lambda i: (0, ch(i), P_SMALL // 128)),
                  pl.BlockSpec((nb, 32, CHUNK), lambda i: (0, 0, ch(i))),
                  pl.BlockSpec((1, 2, 128), lambda i: (d, 0, 0)),
                  pl.BlockSpec((1, 8, 2), lambda i: (d, 0, 0))],
        out_specs=pl.BlockSpec((nb, CHUNK, GW), lambda i: (0, ch(i), 0)),
        scratch_shapes=[pltpu.VMEM((nb * 4, 128, 128), f32)],
        compiler_params=_cparams(("arbitrary",)),
        name="ssd_rev" if reverse else "ssd_fwd",
    )(cv3, cv3, p3, smt3, pc, pr)


TRI_BASE = 8
GDN_GROUP = 16


def _unit_tri_inverse_minus_eye(ms, ii, jj):
    base = (ii // TRI_BASE) == (jj // TRI_BASE)
    pws = _each(lambda m: jnp.where(base, -m, 0.0), ms)
    accs = pws
    size = 2
    while size < TRI_BASE:
        pws = _each(lambda p: _dot(p, p), pws)
        accs = _each(lambda a, p: a + p + _dot(a, p), accs, pws)
        size *= 2
    half = TRI_BASE
    while half < CHUNK:
        off = jnp.logical_and((ii // (2 * half)) == (jj // (2 * half)), (ii // half) != (jj // half))
        mos = _each(lambda m: jnp.where(off, m, 0.0), ms)
        xs = _each(lambda a, mo: mo + _dot(a, mo), accs, mos)
        accs = _each(lambda a, x: a - x - _dot(x, a), accs, xs)
        half *= 2
    return accs


def _gdn_body(q_ref, k_ref, v_ref, sm_ref, smt_ref, pc_ref, pr_ref, o_ref, s_s, *, reverse, d, nb, group):
    i = pl.program_id(0)

    @pl.when(i == 0)
    def _():
        s_s[...] = jnp.zeros_like(s_s)

    mask, strict, blk_i, blk_j = _tri_masks(reverse)
    per_b = []
    for b in range(nb):
        sm = sm_ref[b]
        la_col = pc_ref[0, 1:2, :] * _softplus(sm + pc_ref[0, 0:1, :])
        la_row = pr_ref[0, :, 1:2] * _softplus(smt_ref[b] + pr_ref[0, :, 0:1])
        cum_col, cum_row, _ = _cumsums(la_col, la_row, reverse)
        tot = cum_col[0:1, :] if reverse else cum_col[CHUNK - 1:CHUNK, :]
        per_b.append((cum_col, cum_row, tot, _sigmoid(sm)))

    chains = [(b, h) for b in range(nb) for h in range(GDN_HEADS)]
    for g0 in range(0, len(chains), group):
        grp = chains[g0:g0 + group]

        def lanes(h):
            return slice(h * HEAD_DIM, (h + 1) * HEAD_DIM)

        def l2n(x):
            return x * lax.rsqrt(jnp.sum(x * x, axis=-1, keepdims=True) + 1e-6)

        ca = [8 + d * GDN_HEADS + h for _, h in grp]
        cb = [16 + d * GDN_HEADS + h for _, h in grp]
        qn = [l2n(q_ref[b, :, lanes(h)]) * (HEAD_DIM ** -0.5) for b, h in grp]
        kn = [l2n(k_ref[b, :, lanes(h)]) for b, h in grp]
        vh = [v_ref[b, :, lanes(h)] for b, h in grp]
        cc = [per_b[b][0][:, c:c + 1] for (b, _), c in zip(grp, ca)]
        cr = [per_b[b][1][c:c + 1, :] for (b, _), c in zip(grp, ca)]
        th = [per_b[b][2][:, c:c + 1] for (b, _), c in zip(grp, ca)]
        beta = [per_b[b][3][:, c:c + 1] for (b, _), c in zip(grp, cb)]
        st = [s_s[b * GDN_HEADS + h] for b, h in grp]
        dec = _each(lambda c, r: jnp.where(mask, jnp.exp(c - r), 0.0), cc, cr)
        kb = _each(lambda k, bt: k * bt, kn, beta)
        ms = _each(lambda a, k, dc: jnp.where(strict, _dot_nt(a, k) * dc, 0.0), kb, kn, dec)
        attn = _each(lambda q, k, dc: _dot_nt(q, k) * dc, qn, kn, dec)
        qs = _each(lambda q, c, s: _dot(q * jnp.exp(c), s), qn, cc, st)
        accs = _unit_tri_inverse_minus_eye(ms, blk_i, blk_j)
        rhs = _each(lambda v, bt, k, c: jnp.concatenate([v * bt, k * jnp.exp(c)], axis=1), vh, beta, kb, cc)
        sol = _each(lambda r, a: r + _dot(a, r), rhs, accs)
        v_new = _each(lambda s_, s: s_[:, :HEAD_DIM] - _dot(s_[:, HEAD_DIM:], s), sol, st)
        outs = _each(lambda o, a, v: o + _dot(a, v), qs, attn, v_new)
        new_s = _each(lambda t, s, k, c, v: jnp.exp(t) * s + _dot_tn(k * jnp.exp(t - c), v), th, st, kn, cc, v_new)
        for (b, h), o, s in zip(grp, outs, new_s):
            o_ref[b, :, lanes(h)] = o
            s_s[b * GDN_HEADS + h] = s


def _gdn_call(cv3, p3, smt3, pc, pr, lc, d, reverse):
    nb, seq, _ = cv3.shape
    nc = seq // CHUNK
    nctx = lc // CHUNK

    def ch(i):
        return _chunk_order(i, nctx, nc, reverse)

    def cspec(col):
        return pl.BlockSpec((nb, CHUNK, GW), lambda i: (0, ch(i), col))

    return pl.pallas_call(
        functools.partial(_gdn_body, reverse=reverse, d=d, nb=nb, group=GDN_GROUP),
        out_shape=jax.ShapeDtypeStruct((nb, seq, GW), f32),
        grid=(nc,),
        in_specs=[cspec(0), cspec(1), cspec(2),
                  pl.BlockSpec((nb, CHUNK, 128), lambda i: (0, ch(i), P_SMALL // 128)),
                  pl.BlockSpec((nb, 32, CHUNK), lambda i: (0, 0, ch(i))),
                  pl.BlockSpec((1, 2, 128), lambda i: (d, 0, 0)),
                  pl.BlockSpec((1, 32, 2), lambda i: (d, 0, 0))],
        out_specs=pl.BlockSpec((nb, CHUNK, GW), lambda i: (0, ch(i), 0)),
        scratch_shapes=[pltpu.VMEM((nb * GDN_HEADS, HEAD_DIM, HEAD_DIM), f32)],
        compiler_params=_cparams(("arbitrary",)),
        name="gdn_rev" if reverse else "gdn_fwd",
    )(cv3, cv3, cv3, p3, smt3, pc, pr)


def _rms_lanes(x, w, eps):
    return x * lax.rsqrt(jnp.mean(x * x, axis=-1, keepdims=True) + eps) * w


def _finish_body(hf, hb, rf, rb, sf, sb, gf, gb, gate, rg, cz, dz, xs, retw, ssdd, ssdw, gdnw, o_ref):
    g = gate[...]
    gelu = 0.5 * g * (1.0 + jnp.tanh(math.sqrt(2.0 / math.pi) * (g + 0.044715 * (g * g * g))))
    o_ref[:, 0:GW] = (gelu * (hf[...] + hb[...])).astype(bf16)

    ro = rf[...] + rb[...]
    rgate = _silu(rg[...])
    go = gf[...] + gb[...]
    ggate = _silu(dz[...])
    for h in range(RET_HEADS):
        lanes = slice(h * HEAD_DIM, (h + 1) * HEAD_DIM)
        o_ref[:, GW + h * HEAD_DIM:GW + (h + 1) * HEAD_DIM] = (
            rgate[:, lanes] * _rms_lanes(ro[:, lanes], retw[:, lanes], 1e-6)).astype(bf16)
        o_ref[:, 3 * GW + h * HEAD_DIM:3 * GW + (h + 1) * HEAD_DIM] = (
            _rms_lanes(go[:, lanes], gdnw[:, lanes], 1e-6) * ggate[:, lanes]).astype(bf16)

    sy = (sf[...] + sb[...] + ssdd[...] * xs[...]) * _silu(cz[...])
    o_ref[:, 2 * GW:3 * GW] = _rms_lanes(sy, ssdw[...], 1e-6).astype(bf16)


def _finish_call(scans, p, cv, retw, ssdd, ssdw, gdnw):
    n = p.shape[0]
    tm = ROW_TILE

    def rowspec(col):
        return pl.BlockSpec((tm, GW), lambda i: (i, col))

    vec = pl.BlockSpec((1, GW), lambda i: (0, 0))
    return pl.pallas_call(
        _finish_body,
        out_shape=jax.ShapeDtypeStruct((n, D_MODEL), bf16),
        grid=(n // tm,),
        in_specs=[rowspec(0)] * 8 + [rowspec(P_GATE // GW), rowspec(P_G // GW), rowspec(P_CZ // GW),
                                     rowspec(P_DZ // GW), rowspec(4), vec, vec, vec, vec],
        out_specs=pl.BlockSpec((tm, D_MODEL), lambda i: (i, 0)),
        compiler_params=_cparams(("arbitrary",)),
        name="mixer_finish",
    )(*scans, p, p, p, p, cv, retw, ssdd, ssdw, gdnw)


def _layer_norm_rows(t, w, b):
    mu = jnp.mean(t, axis=-1, keepdims=True)
    tc = t - mu
    var = jnp.mean(tc * tc, axis=-1, keepdims=True)
    return tc * lax.rsqrt(var + 1e-5) * w + b


HALF = D_MODEL // 2
HI_MASK = 0xFFFF0000


def _pack_bf16_pair(lo, hi):
    lo_bits = lax.bitcast_convert_type(lo.astype(bf16).astype(f32), jnp.uint32)
    hi_bits = lax.bitcast_convert_type(hi.astype(bf16).astype(f32), jnp.uint32)
    return hi_bits | (lo_bits >> 16)


def _unpack_bf16_pair(w):
    lo = lax.bitcast_convert_type(w << 16, f32)
    hi = lax.bitcast_convert_type(w & jnp.uint32(HI_MASK), f32)
    return lo, hi


TOKEN_ROWS = HALF // 128


def _store_token_tiles(ref, packed):
    m = packed.shape[0]
    for c in range(TOKEN_ROWS):
        ref[pl.ds(c, m, stride=TOKEN_ROWS), :] = packed[:, c * 128:(c + 1) * 128]


def _load_token_tiles(ref, m):
    return jnp.concatenate([ref[pl.ds(c, m, stride=TOKEN_ROWS), :] for c in range(TOKEN_ROWS)], axis=1)


def _split_bf16(x):
    hi = x.astype(bf16)
    return hi, (x - hi.astype(f32)).astype(bf16)


def _outproj_body(y_ref, w_ref, x_ref, g1_ref, lw_ref, lb_ref, sc_ref, sh_ref, rwh_ref, rwl_ref,
                  x1_ref, u_ref, lg_ref, *, alpha):
    y = jnp.dot(y_ref[...], w_ref[...], preferred_element_type=f32)
    x1 = _layer_norm_rows(alpha * x_ref[...] + g1_ref[0] * y, lw_ref[...], lb_ref[...])
    x1_ref[...] = x1
    u = x1 * (1.0 + sc_ref[0]) + sh_ref[0]
    _store_token_tiles(u_ref, _pack_bf16_pair(u[:, :HALF], u[:, HALF:]))
    uh, ul = _split_bf16(u)
    rwh = rwh_ref[...]
    lg = (jnp.dot(uh, rwh, preferred_element_type=f32) + jnp.dot(ul, rwh, preferred_element_type=f32)
          + jnp.dot(uh, rwl_ref[...], preferred_element_type=f32))
    lg_ref[...] = lg.T[:N_EXPERTS, :]


def _outproj_call(ycat, w_bf, x, g1, lw, lb, sc2, sh2, rw_hi, rw_lo, alpha):
    n, d = x.shape
    tm = ROW_TILE
    mod = pl.BlockSpec((1, 1, d), lambda i: (i, 0, 0))
    vec = pl.BlockSpec((1, d), lambda i: (0, 0))
    rws = pl.BlockSpec((d, 2 * N_EXPERTS), lambda i: (0, 0))
    return pl.pallas_call(
        functools.partial(_outproj_body, alpha=alpha),
        out_shape=(jax.ShapeDtypeStruct((n, d), f32), jax.ShapeDtypeStruct((n * TOKEN_ROWS, 128), jnp.uint32),
                   jax.ShapeDtypeStruct((N_EXPERTS, n), f32)),
        grid=(n // tm,),
        in_specs=[pl.BlockSpec((tm, d), lambda i: (i, 0)),
                  pl.BlockSpec((d, d), lambda i: (0, 0)),
                  pl.BlockSpec((tm, d), lambda i: (i, 0)),
                  mod, vec, vec, mod, mod, rws, rws],
        out_specs=(pl.BlockSpec((tm, d), lambda i: (i, 0)), pl.BlockSpec((tm * TOKEN_ROWS, 128), lambda i: (i, 0)),
                   pl.BlockSpec((N_EXPERTS, tm), lambda i: (0, i))),
        compiler_params=_cparams(("arbitrary",)),
        name="outproj_ln1",
    )(ycat, w_bf, x, g1, lw, lb, sc2, sh2, rw_hi, rw_lo)


EXPERT_TILE = 512


def _num_expert_tiles(n_pairs):
    return (n_pairs + N_EXPERTS * (EXPERT_TILE - 1) + EXPERT_TILE - 1) // EXPERT_TILE


def _route_rank_body(lg_ref, bias_ref, idx_ref, wt_ref, rank_ref, cnt_ref, carry_s):
    i = pl.program_id(0)

    @pl.when(i == 0)
    def _():
        carry_s[...] = jnp.zeros_like(carry_s)

    tn = lg_ref.shape[1]
    scores = _sigmoid(lg_ref[...])
    sel = scores + bias_ref[...]
    gsz = N_EXPERTS // N_EXPERT_GROUPS
    neg = -jnp.inf
    sel3 = sel.reshape(N_EXPERT_GROUPS, gsz, tn)
    io3 = lax.broadcasted_iota(jnp.int32, sel3.shape, 1)
    m1 = jnp.max(sel3, axis=1)
    first = jnp.min(jnp.where(sel3 == m1[:, None, :], io3, gsz), axis=1)
    m2 = jnp.max(jnp.where(io3 == first[:, None, :], neg, sel3), axis=1)
    gscore = m1 + m2
    iog = lax.broadcasted_iota(jnp.int32, gscore.shape, 0)
    gsel = None
    for _ in range(TOPK_GROUPS):
        gm = jnp.max(gscore, axis=0, keepdims=True)
        gi = jnp.min(jnp.where(gscore == gm, iog, N_EXPERT_GROUPS), axis=0, keepdims=True)
        hit = iog == gi
        gsel = hit if gsel is None else jnp.logical_or(gsel, hit)
        gscore = jnp.where(hit, neg, gscore)
    emask = jnp.broadcast_to(gsel[:, None, :], sel3.shape).reshape(N_EXPERTS, tn)
    cand = jnp.where(emask, sel, neg)
    ioe = lax.broadcasted_iota(jnp.int32, cand.shape, 0)
    idxs = []
    wts = []
    hits = []
    for _ in range(TOP_K):
        cm = jnp.max(cand, axis=0, keepdims=True)
        ci = jnp.min(jnp.where(cand == cm, ioe, N_EXPERTS), axis=0, keepdims=True)
        hit = ioe == ci
        idxs.append(ci)
        hits.append(hit)
        wts.append(jnp.sum(jnp.where(hit, scores, 0.0), axis=0, keepdims=True))
        cand = jnp.where(hit, neg, cand)
    wsum = wts[0]
    for w in wts[1:]:
        wsum = wsum + w
    idx_ref[...] = jnp.concatenate(idxs, axis=0)
    wnorm = jnp.concatenate([w / wsum * ROUTED_SCALE for w in wts], axis=0)
    wt_ref[...] = jnp.concatenate([wnorm, jnp.zeros((128 - TOP_K, tn), f32)], axis=0).T

    r_io = lax.broadcasted_iota(jnp.int32, (tn, tn), 0)
    c_io = lax.broadcasted_iota(jnp.int32, (tn, tn), 1)
    prefix = jnp.where(r_io <= c_io, 1.0, 0.0).astype(bf16)
    base = carry_s[...]
    ranks = []
    for hit in hits:
        onehot = jnp.where(hit, 1.0, 0.0)
        cum = jnp.dot(onehot.astype(bf16), prefix, preferred_element_type=f32)
        ranks.append(jnp.sum(onehot * (cum - 1.0 + base), axis=0, keepdims=True))
        base = base + cum[:, tn - 1:tn]
    carry_s[...] = base
    rank_ref[...] = jnp.concatenate(ranks, axis=0).astype(jnp.int32)
    cnt_ref[...] = jnp.broadcast_to(base, cnt_ref.shape)


def _route_rank_call(logits_t, bias):
    n = logits_t.shape[1]
    tn = ROW_TILE
    kt = pl.BlockSpec((TOP_K, tn), lambda i: (0, i))
    return pl.pallas_call(
        _route_rank_body,
        out_shape=(jax.ShapeDtypeStruct((TOP_K, n), jnp.int32),
                   jax.ShapeDtypeStruct((n, 128), f32),
                   jax.ShapeDtypeStruct((TOP_K, n), jnp.int32),
                   jax.ShapeDtypeStruct((N_EXPERTS, 128), f32)),
        grid=(n // tn,),
        in_specs=[pl.BlockSpec((N_EXPERTS, tn), lambda i: (0, i)),
                  pl.BlockSpec((N_EXPERTS, 1), lambda i: (0, 0))],
        out_specs=(kt, pl.BlockSpec((tn, 128), lambda i: (i, 0)), kt,
                   pl.BlockSpec((N_EXPERTS, 128), lambda i: (0, 0))),
        scratch_shapes=[pltpu.VMEM((N_EXPERTS, 1), f32)],
        compiler_params=_cparams(("arbitrary",)),
        name="router_rank",
    )(logits_t, bias)


def _expert_layout(counts, nt):
    tiles_e = (counts + EXPERT_TILE - 1) // EXPERT_TILE
    tend = jnp.cumsum(tiles_e)
    tstart = tend - tiles_e
    n_used = tend[-1:].astype(jnp.int32)
    tile = jnp.arange(nt, dtype=jnp.int32)
    te = jnp.minimum(jnp.sum((tend[None, :] <= tile[:, None]).astype(jnp.int32), axis=1), N_EXPERTS - 1)
    last_tile = jnp.where(tiles_e > 0, tend - 1, -1).astype(jnp.int32)
    pstart = (tstart * EXPERT_TILE).astype(f32)[:, None]
    return te.astype(jnp.int32), n_used, last_tile, pstart


def _dispatch_body(last_ref, nu_ref, u_ref, idx_ref, rank_ref, pstart_ref, pos_ref, xs_hbm,
                   zbuf, pos_s, zsem, psem, ssem):
    i = pl.program_id(0)
    tn = idx_ref.shape[1]
    tr = TOKEN_ROWS
    tile_rows = EXPERT_TILE * tr
    n_tiles = xs_hbm.shape[0] // tile_rows

    @pl.when(i == 0)
    def _():
        zbuf[...] = jnp.zeros_like(zbuf)

        def zero_tile(t):
            row = pl.multiple_of(t * tile_rows, tile_rows)
            return pltpu.make_async_copy(zbuf, xs_hbm.at[pl.ds(row, tile_rows)], zsem)

        for e in range(N_EXPERTS):
            @pl.when(last_ref[e] >= 0)
            def _(e=e):
                zero_tile(last_ref[e]).start()
        lax.fori_loop(nu_ref[0], n_tiles, lambda t, c: (zero_tile(t).start(), c)[1], 0)
        for e in range(N_EXPERTS):
            @pl.when(last_ref[e] >= 0)
            def _():
                zero_tile(0).wait()
        lax.fori_loop(nu_ref[0], n_tiles, lambda t, c: (zero_tile(0).wait(), c)[1], 0)

    ioe = lax.broadcasted_iota(jnp.int32, (N_EXPERTS, tn), 0)
    pstart = pstart_ref[...]
    pos = []
    for k in range(TOP_K):
        start_k = jnp.sum(jnp.where(ioe == idx_ref[k:k + 1, :], pstart, 0.0), axis=0, keepdims=True)
        pos.append(start_k.astype(jnp.int32) + rank_ref[k:k + 1, :])
    pos_ref[...] = jnp.concatenate(pos, axis=0)
    to_smem = pltpu.make_async_copy(pos_ref, pos_s, psem)
    to_smem.start()
    to_smem.wait()

    def per_token(t, carry):
        src = u_ref.at[pl.ds(pl.multiple_of(t * tr, tr), tr)]
        for k in range(TOP_K):
            row = pl.multiple_of(pos_s[k, t] * tr, tr)
            pltpu.make_async_copy(src, xs_hbm.at[pl.ds(row, tr)], ssem).start(priority=k % 2)
        return carry

    lax.fori_loop(0, tn, per_token, 0, unroll=4)
    for k in range(TOP_K):
        pltpu.make_async_copy(u_ref, xs_hbm.at[pl.ds(0, tn * tr)], ssem).wait()


def _dispatch_call(last_tile, n_used, u_pk, idx, rank, pstart, nt):
    n = idx.shape[1]
    tn = ROW_TILE
    tr = TOKEN_ROWS
    kt = pl.BlockSpec((TOP_K, tn), lambda i, last, nu: (0, i))
    grid_spec = pltpu.PrefetchScalarGridSpec(
        num_scalar_prefetch=2,
        grid=(n // tn,),
        in_specs=[pl.BlockSpec((tn * tr, 128), lambda i, last, nu: (i, 0)), kt, kt,
                  pl.BlockSpec((N_EXPERTS, 1), lambda i, last, nu: (0, 0))],
        out_specs=(kt, pl.BlockSpec(memory_space=pl.ANY)),
        scratch_shapes=[pltpu.VMEM((EXPERT_TILE * tr, 128), jnp.uint32),
                        pltpu.SMEM((TOP_K, tn), jnp.int32),
                        pltpu.SemaphoreType.DMA(()), pltpu.SemaphoreType.DMA(()), pltpu.SemaphoreType.DMA(())])
    return pl.pallas_call(
        _dispatch_body,
        out_shape=(jax.ShapeDtypeStruct((TOP_K, n), jnp.int32),
                   jax.ShapeDtypeStruct((nt * EXPERT_TILE * tr, 128), jnp.uint32)),
        grid_spec=grid_spec,
        compiler_params=_cparams(("arbitrary",)),
        name="dispatch",
    )(last_tile, n_used, u_pk, idx, rank, pstart)


def _sorted_experts_body(te_ref, nu_ref, x_ref, w1_ref, w3_ref, w2_ref, o_ref, w1_s, w3_s, w2_s):
    i = pl.program_id(0)
    nu = nu_ref[0]
    tm = EXPERT_TILE

    @pl.when(i < nu)
    def _():
        prev = te_ref[jnp.maximum(i - 1, 0)]

        @pl.when(jnp.logical_or(i == 0, te_ref[i] != prev))
        def _():
            w1_s[...] = w1_ref[0, 0].astype(bf16)
            w3_s[...] = w3_ref[0, 0].astype(bf16)
            w2_s[...] = w2_ref[0, 0].astype(bf16)

        lo, hi = _unpack_bf16_pair(_load_token_tiles(x_ref, tm))
        lo = lo.astype(bf16)
        hi = hi.astype(bf16)
        h1 = (jnp.dot(lo, w1_s[:HALF, :], preferred_element_type=f32)
              + jnp.dot(hi, w1_s[HALF:, :], preferred_element_type=f32))
        h3 = (jnp.dot(lo, w3_s[:HALF, :], preferred_element_type=f32)
              + jnp.dot(hi, w3_s[HALF:, :], preferred_element_type=f32))
        y = jnp.dot((_silu(h1) * h3).astype(bf16), w2_s[...], preferred_element_type=f32)
        _store_token_tiles(o_ref, _pack_bf16_pair(y[:, :HALF], y[:, HALF:]))

    @pl.when(i >= nu)
    def _():
        o_ref[...] = jnp.zeros_like(o_ref)


def _sorted_experts_call(te, n_used, xs, w1, w3, w2, layer):
    nt = te.shape[0]
    _, ne, d, fe = w1.shape
    tm = EXPERT_TILE
    rows = tm * TOKEN_ROWS

    def wmap(i, te_r, nu_r):
        return (layer, te_r[jnp.minimum(i, nu_r[0] - 1)], 0, 0)

    grid_spec = pltpu.PrefetchScalarGridSpec(
        num_scalar_prefetch=2,
        grid=(nt,),
        in_specs=[pl.BlockSpec((rows, 128), lambda i, te_r, nu_r: (jnp.minimum(i, nu_r[0] - 1), 0)),
                  pl.BlockSpec((1, 1, d, fe), wmap), pl.BlockSpec((1, 1, d, fe), wmap),
                  pl.BlockSpec((1, 1, fe, d), wmap)],
        out_specs=pl.BlockSpec((rows, 128), lambda i, te_r, nu_r: (i, 0)),
        scratch_shapes=[pltpu.VMEM((d, fe), bf16), pltpu.VMEM((d, fe), bf16), pltpu.VMEM((fe, d), bf16)])
    return pl.pallas_call(
        _sorted_experts_body,
        out_shape=jax.ShapeDtypeStruct(xs.shape, jnp.uint32),
        grid_spec=grid_spec,
        compiler_params=_cparams(("arbitrary",)),
        name="experts_sorted",
    )(te, n_used, xs, w1, w3, w2)


def _gather_combine_body(pos_ref, posn_ref, wt_ref, u_ref, x_ref, g2_ref, lw_ref, lb_ref, s1_ref, s3_ref, s2_ref,
                         ys_hbm, o_ref, pos_s, gbuf, psem, gsem, *, alpha):
    i = pl.program_id(0)
    tn = x_ref.shape[0]
    tr = TOKEN_ROWS
    slot = lax.rem(i, 2)
    nslot = 1 - slot

    def start_gathers(src_pos_ref, s):
        to_smem = pltpu.make_async_copy(src_pos_ref, pos_s.at[s], psem)
        to_smem.start()
        to_smem.wait()
        for t in range(tn):
            for k in range(TOP_K):
                row = pl.multiple_of(pos_s[s, k, t] * tr, tr)
                pltpu.make_async_copy(ys_hbm.at[pl.ds(row, tr)], gbuf.at[s, k, pl.ds(t * tr, tr)],
                                      gsem.at[s]).start(priority=k % 2)

    @pl.when(i == 0)
    def _():
        start_gathers(pos_ref, 0)

    @pl.when(i + 1 < pl.num_programs(0))
    def _():
        start_gathers(posn_ref, nslot)

    ulo, uhi = _unpack_bf16_pair(_load_token_tiles(u_ref, tn))
    ulo = ulo.astype(bf16)
    uhi = uhi.astype(bf16)
    a1 = (jnp.dot(ulo, s1_ref[:HALF, :], preferred_element_type=f32)
          + jnp.dot(uhi, s1_ref[HALF:, :], preferred_element_type=f32))
    a3 = (jnp.dot(ulo, s3_ref[:HALF, :], preferred_element_type=f32)
          + jnp.dot(uhi, s3_ref[HALF:, :], preferred_element_type=f32))
    shared = jnp.dot((_silu(a1) * a3).astype(bf16), s2_ref[...], preferred_element_type=f32)

    for k in range(TOP_K):
        pltpu.make_async_copy(ys_hbm.at[pl.ds(0, tn * tr)], gbuf.at[slot, k], gsem.at[slot]).wait()
    wt = wt_ref[...]
    lo = None
    hi = None
    for k in range(TOP_K):
        lo_k, hi_k = _unpack_bf16_pair(_load_token_tiles(gbuf.at[slot, k], tn))
        gate = wt[:, k:k + 1]
        lo = gate * lo_k if lo is None else lo + gate * lo_k
        hi = gate * hi_k if hi is None else hi + gate * hi_k
    f = jnp.concatenate([lo, hi], axis=1) + shared
    o_ref[...] = _layer_norm_rows(alpha * x_ref[...] + g2_ref[0] * f, lw_ref[...], lb_ref[...])


def _gather_combine_call(pos, wt_tm, u_pk, x1, g2, lw, lb, s1, s3, s2, ys, alpha):
    n, d = x1.shape
    tn = ROW_TILE
    tr = TOKEN_ROWS
    fe = s1.shape[1]
    row = pl.BlockSpec((tn, d), lambda i: (i, 0))
    vec = pl.BlockSpec((1, d), lambda i: (0, 0))
    last = n // tn - 1
    return pl.pallas_call(
        functools.partial(_gather_combine_body, alpha=alpha),
        out_shape=jax.ShapeDtypeStruct((n, d), f32),
        grid=(n // tn,),
        in_specs=[pl.BlockSpec((TOP_K, tn), lambda i: (0, i)),
                  pl.BlockSpec((TOP_K, tn), lambda i: (0, jnp.minimum(i + 1, last))),
                  pl.BlockSpec((tn, 128), lambda i: (i, 0)),
                  pl.BlockSpec((tn * tr, 128), lambda i: (i, 0)), row,
                  pl.BlockSpec((1, 1, d), lambda i: (i, 0, 0)), vec, vec,
                  pl.BlockSpec((d, fe), lambda i: (0, 0)), pl.BlockSpec((d, fe), lambda i: (0, 0)),
                  pl.BlockSpec((fe, d), lambda i: (0, 0)),
                  pl.BlockSpec(memory_space=pl.ANY)],
        out_specs=row,
        scratch_shapes=[pltpu.SMEM((2, TOP_K, tn), jnp.int32),
                        pltpu.VMEM((2, TOP_K, tn * tr, 128), jnp.uint32),
                        pltpu.SemaphoreType.DMA(()), pltpu.SemaphoreType.DMA((2,))],
        compiler_params=_cparams(("arbitrary",)),
        name="gather_combine_ln2",
    )(pos, pos, wt_tm, u_pk, x1, g2, lw, lb, s1, s3, s2, ys)


def _block_diag(w):
    g, n, _ = w.shape
    eye = jnp.eye(g, dtype=w.dtype)
    return (eye[:, None, :, None] * w[:, :, None, :]).reshape(g * n, g * n)


def _rope_tables(seq, lc):
    half = HEAD_DIM // 2
    t = jnp.arange(seq - lc, dtype=f32)
    inv = ROPE_BASE ** (-jnp.arange(0, half, 2, dtype=f32) / half)
    ar = jnp.floor(t / GRID_W)[:, None] * inv[None, :]
    ac = (t - jnp.floor(t / GRID_W) * GRID_W)[:, None] * inv[None, :]
    cos = jnp.concatenate([jnp.cos(ar), jnp.cos(ar), jnp.cos(ac), jnp.cos(ac)], axis=1)
    sin = jnp.concatenate([-jnp.sin(ar), jnp.sin(ar), -jnp.sin(ac), jnp.sin(ac)], axis=1)
    cos = jnp.concatenate([jnp.ones((lc, HEAD_DIM), f32), cos], axis=0)
    sin = jnp.concatenate([jnp.zeros((lc, HEAD_DIM), f32), sin], axis=0)
    return cos, sin


def _pad_lanes(v, offset, width=128):
    return jnp.zeros((width,), f32).at[offset:offset + v.shape[0]].set(v)


def kernel(x, c, ctx, c_ctx, w_ada, b_ada, w_in, lru_conv_w, lru_conv_b, lru_gate_a_w, lru_gate_a_b,
           lru_gate_x_w, lru_gate_x_b, lru_lambda, ret_log_decay, ret_norm_w, ssd_conv_w, ssd_conv_b,
           ssd_dt_bias, ssd_a_log, ssd_d, ssd_norm_w, gdn_conv_w, gdn_dt_bias, gdn_a_log, gdn_norm_w,
           w_out, ln1_w, ln1_b, router_w, router_bias, exp_w1, exp_w3, exp_w2, sh_w1, sh_w3, sh_w2,
           ln2_w, ln2_b):
    nb, lat, d = x.shape
    lc = ctx.shape[1]
    depth = w_ada.shape[0]
    seq = lc + lat
    n = nb * seq
    ntile = n // ROW_TILE
    tiles_per_seq = seq // ROW_TILE
    alpha = (2 * depth) ** 0.25
    assert d == D_MODEL and lc == ROW_TILE and lat % ROW_TILE == 0 and nb <= 7

    xall = jnp.concatenate([ctx, x], axis=1).reshape(n, d)

    svec = jnp.zeros((8, d), f32).at[0].set(c_ctx).at[1:1 + nb].set(c)
    modtab = _ada_call(svec, w_ada, b_ada).reshape(depth, 8, 6, d)
    tile = jnp.arange(ntile)
    tile_row = jnp.where(tile % tiles_per_seq == 0, 0, 1 + tile // tiles_per_seq)
    cos, sin = _rope_tables(seq, lc)

    for l in range(depth):
        mods = [modtab[l, :, k, :][tile_row][:, None, :] for k in range(6)]
        sh1, sc1, g1, sh2, sc2, g2 = mods

        wl = w_in[l]
        w_re = jnp.concatenate([wl[:, :3584], wl[:, 5896:6408], wl[:, 4360:5896], wl[:, 3584:4352],
                                wl[:, 4352:4360], wl[:, 6408:6424],
                                jnp.zeros((d, IN_COLS_PAD - 6424), f32)], axis=1).astype(bf16)
        p = _inproj_call(xall, sc1, sh1, w_re)

        wcat = jnp.concatenate([gdn_conv_w[l], lru_conv_w[l], ssd_conv_w[l]], axis=1)
        bcat = jnp.concatenate([jnp.zeros((3 * GW,), f32), lru_conv_b[l], ssd_conv_b[l]])[None, :]
        cv = _conv_call(p, wcat, bcat, nb, seq, lc)
        p3 = p.reshape(nb, seq, IN_COLS_PAD)
        cv3 = cv.reshape(nb, seq, CV_COLS)
        smt3 = jnp.transpose(p3[:, :, P_SMALL:P_SMALL + 32], (0, 2, 1))

        wg = jnp.stack([jnp.concatenate([_block_diag(lru_gate_a_w[l, dd]), _block_diag(lru_gate_x_w[l, dd])],
                                        axis=1) for dd in range(2)]).astype(bf16)
        bg = jnp.concatenate([lru_gate_a_b[l], lru_gate_x_b[l]], axis=1)[:, None, :]
        sp = jax.nn.softplus(-lru_lambda[l])[:, None, :]
        hs = [_lru_call(cv3, wg, bg, sp, lc, dd, bool(dd)).reshape(n, GW) for dd in range(2)]

        lg = -jnp.exp(ret_log_decay[l].astype(f32))
        rs = [_ret_call(p3, lg[dd], cos, sin, lc, bool(dd)).reshape(n, GW) for dd in range(2)]

        a_ssd = -jnp.exp(ssd_a_log[l])
        pc = jnp.stack([jnp.stack([_pad_lanes(ssd_dt_bias[l, dd], 0), _pad_lanes(a_ssd[dd], 0)])
                        for dd in range(2)])
        pr = jnp.stack([jnp.stack([ssd_dt_bias[l, dd], a_ssd[dd]], axis=1) for dd in range(2)])
        ss = [_ssd_call(cv3, p3, smt3, pc, pr, lc, dd, bool(dd)).reshape(n, GW) for dd in range(2)]

        a_gdn = -jnp.exp(gdn_a_log[l])
        pcg = jnp.stack([jnp.stack([_pad_lanes(gdn_dt_bias[l, dd], 8 + 4 * dd),
                                    _pad_lanes(a_gdn[dd], 8 + 4 * dd)]) for dd in range(2)])
        prg = jnp.stack([jnp.stack([_pad_lanes(gdn_dt_bias[l, dd], 8 + 4 * dd, 32),
                                    _pad_lanes(a_gdn[dd], 8 + 4 * dd, 32)], axis=1) for dd in range(2)])
        gs = [_gdn_call(cv3, p3, smt3, pcg, prg, lc, dd, bool(dd)).reshape(n, GW) for dd in range(2)]

        ycat = _finish_call(hs + rs + ss + gs, p, cv, ret_norm_w[l][None, :],
                            jnp.repeat(ssd_d[l], SSD_HEAD_DIM)[None, :], ssd_norm_w[l][None, :],
                            jnp.tile(gdn_norm_w[l], GDN_HEADS)[None, :])

        rw = jnp.concatenate([router_w[l], jnp.zeros((d, N_EXPERTS), f32)], axis=1)
        rw_hi = rw.astype(bf16)
        rw_lo = (rw - rw_hi.astype(f32)).astype(bf16)
        x1, u_pk, logits_t = _outproj_call(ycat, w_out[l].astype(bf16), xall, g1, ln1_w[l][None, :],
                                           ln1_b[l][None, :], sc2, sh2, rw_hi, rw_lo, alpha)
        idx, wt_tm, rank, cnt = _route_rank_call(logits_t, router_bias[l][:, None])
        nt = _num_expert_tiles(TOP_K * n)
        te, n_used, last_tile, pstart = _expert_layout(cnt[:, 0].astype(jnp.int32), nt)
        pos, xs = _dispatch_call(last_tile, n_used, u_pk, idx, rank, pstart, nt)
        ys = _sorted_experts_call(te, n_used, xs, exp_w1, exp_w3, exp_w2, l)
        xall = _gather_combine_call(pos, wt_tm, u_pk, x1, g2, ln2_w[l][None, :], ln2_b[l][None, :],
                                    sh_w1[l].astype(bf16), sh_w3[l].astype(bf16), sh_w2[l].astype(bf16),
                                    ys, alpha)

    return xall.reshape(nb, seq, d)[:, lc:, :]
```

```python
import functools
import math

import jax
import jax.numpy as jnp
from jax import lax
from jax.experimental import pallas as pl
from jax.experimental.pallas import tpu as pltpu

f32 = jnp.float32
bf16 = jnp.bfloat16
HIGHEST = lax.Precision.HIGHEST

D_MODEL = 2048
GW = 512
GRID_W = 64
CONV_K = 4
LRU_C = 8.0
RET_HEADS = 4
HEAD_DIM = 128
ROPE_BASE = 10000.0
SSD_HEADS = 8
SSD_HEAD_DIM = 64
SSD_STATE = 64
GDN_HEADS = 4
N_EXPERTS = 64
TOP_K = 8
N_EXPERT_GROUPS = 8
TOPK_GROUPS = 4
D_EXPERT = 256
ROUTED_SCALE = 2.5

ROW_TILE = 256
CHUNK = 128
IN_COLS_PAD = 6528
CV_COLS = 2816
VMEM_LIMIT = 56 * 1024 * 1024

P_XB, P_GATE, P_Q, P_K, P_V, P_G, P_CZ, P_DZ, P_DQKV, P_XBC, P_SMALL = (
    0, 512, 1024, 1536, 2048, 2560, 3072, 3584, 4096, 5632, 6400)


def _cparams(sem):
    return pltpu.CompilerParams(dimension_semantics=sem, vmem_limit_bytes=VMEM_LIMIT)


def _sigmoid(x):
    return 1.0 / (1.0 + jnp.exp(-x))


def _silu(x):
    return x * _sigmoid(x)


def _softplus(x):
    return jnp.maximum(x, 0.0) + jnp.log(1.0 + jnp.exp(-jnp.abs(x)))


def _dot(a, b):
    return jnp.dot(a.astype(bf16), b.astype(bf16), preferred_element_type=f32)


def _dot_nt(a, b):
    return lax.dot_general(a.astype(bf16), b.astype(bf16), (((1,), (1,)), ((), ())),
                           preferred_element_type=f32)


def _dot_tn(a, b):
    return lax.dot_general(a.astype(bf16), b.astype(bf16), (((0,), (0,)), ((), ())),
                           preferred_element_type=f32)


def _chunk_order(i, n_ctx, n_all, reverse):
    if not reverse:
        return i
    return jnp.where(i < n_ctx, n_ctx - 1 - i, n_all + n_ctx - 1 - i)


def _ada_body(s_ref, w_ref, b_ref, o_ref):
    xh, xl = _split_bf16(_silu(s_ref[...]))
    wh, wl = _split_bf16(w_ref[0])
    acc = jnp.dot(xh, wh, preferred_element_type=f32) + jnp.dot(xl, wh, preferred_element_type=f32)
    o_ref[0] = acc + jnp.dot(xh, wl, preferred_element_type=f32) + b_ref[0]


def _ada_call(svec, w_ada, b_ada):
    nl, d, d6 = w_ada.shape
    tn = 1024
    return pl.pallas_call(
        _ada_body,
        out_shape=jax.ShapeDtypeStruct((nl, 8, d6), f32),
        grid=(nl, d6 // tn),
        in_specs=[pl.BlockSpec((8, d), lambda l, j: (0, 0)),
                  pl.BlockSpec((1, d, tn), lambda l, j: (l, 0, j)),
                  pl.BlockSpec((1, 1, tn), lambda l, j: (l, 0, j))],
        out_specs=pl.BlockSpec((1, 8, tn), lambda l, j: (l, 0, j)),
        compiler_params=_cparams(("arbitrary", "arbitrary")),
        name="ada",
    )(svec, w_ada, b_ada.reshape(nl, 1, d6))


def _inproj_body(x_ref, sc_ref, sh_ref, w_ref, o_ref, *, sub):
    for s in range(sub):
        rows = slice(s * ROW_TILE, (s + 1) * ROW_TILE)
        xm = x_ref[rows, :] * (1.0 + sc_ref[s]) + sh_ref[s]
        o_ref[rows, :] = _dot(xm, w_ref[...])


def _inproj_call(x, sc, sh, w_bf):
    n, d = x.shape
    sub = 2
    tm = sub * ROW_TILE
    tn = IN_COLS_PAD // 3
    return pl.pallas_call(
        functools.partial(_inproj_body, sub=sub),
        out_shape=jax.ShapeDtypeStruct((n, IN_COLS_PAD), f32),
        grid=(3, n // tm),
        in_specs=[pl.BlockSpec((tm, d), lambda j, i: (i, 0)),
                  pl.BlockSpec((sub, 1, d), lambda j, i: (i, 0, 0)),
                  pl.BlockSpec((sub, 1, d), lambda j, i: (i, 0, 0)),
                  pl.BlockSpec((d, tn), lambda j, i: (0, j))],
        out_specs=pl.BlockSpec((tm, tn), lambda j, i: (i, j)),
        compiler_params=_cparams(("arbitrary", "arbitrary")),
        name="inproj",
    )(x, sc, sh, w_bf)


def _conv_body(x_ref, w_ref, b_ref, o_ref, *, seq, lc):
    j = pl.program_id(1)
    w = w_ref[...]
    bias = b_ref[...]
    use_act = jnp.logical_or(j < 6, j >= 8)
    row = lax.broadcasted_iota(jnp.int32, (CHUNK, 1), 0)

    def chunk(c, carry):
        r0 = pl.multiple_of(c * CHUNK, CHUNK)
        cur = x_ref[pl.ds(r0, CHUNK), :]
        prev = x_ref[pl.ds(pl.multiple_of(jnp.maximum(r0 - 8, 0), 8), 8), :]
        nxt = x_ref[pl.ds(pl.multiple_of(jnp.minimum(r0 + CHUNK, seq - 8), 8), 8), :]
        ext = jnp.concatenate([prev, cur, nxt], axis=0)
        t = r0 + row
        s0 = jnp.where(r0 < lc, 0, lc)
        s1 = jnp.where(r0 < lc, lc, seq)

        def taps(masked):
            acc = jnp.zeros_like(cur) + bias
            for k in range(CONV_K):
                o = k - CONV_K // 2
                seg = ext[8 + o:8 + o + CHUNK]
                if masked:
                    seg = jnp.where(jnp.logical_and(t + o >= s0, t + o < s1), seg, 0.0)
                acc = acc + seg * w[k:k + 1, :]
            o_ref[pl.ds(r0, CHUNK), :] = jnp.where(use_act, _silu(acc), acc)

        at_edge = jnp.logical_or(r0 == s0, r0 + CHUNK == s1)
        lax.cond(at_edge, lambda: taps(True), lambda: taps(False))
        return carry

    lax.fori_loop(0, seq // CHUNK, chunk, 0)


def _conv_call(p, wcat, bcat, nb, seq, lc):
    n = p.shape[0]
    cb = 256

    def in_map(b, j):
        return (b, jnp.where(j < 6, P_DQKV // cb + j, jnp.where(j < 8, j - 6, P_XBC // cb - 8 + j)))

    return pl.pallas_call(
        functools.partial(_conv_body, seq=seq, lc=lc),
        out_shape=jax.ShapeDtypeStruct((n, CV_COLS), f32),
        grid=(nb, CV_COLS // cb),
        in_specs=[pl.BlockSpec((seq, cb), in_map),
                  pl.BlockSpec((CONV_K, cb), lambda b, j: (0, j)),
                  pl.BlockSpec((1, cb), lambda b, j: (0, j))],
        out_specs=pl.BlockSpec((seq, cb), lambda b, j: (b, j)),
        compiler_params=_cparams(("arbitrary", "arbitrary")),
        name="dwconv",
    )(p, wcat, bcat)


def _lru_body(xc_ref, wg_ref, bg_ref, sp_ref, o_ref, a_s, b_s, h_s, *, reverse, nb):
    i = pl.program_id(0)

    @pl.when(i == 0)
    def _():
        h_s[...] = jnp.zeros_like(h_s)

    for b in range(nb):
        xc = xc_ref[b]
        gates = _dot(xc, wg_ref[0]) + bg_ref[0]
        r = _sigmoid(gates[:, :GW])
        ig = _sigmoid(gates[:, GW:])
        a = jnp.exp(-LRU_C * r * sp_ref[0])
        a_s[b] = a
        b_s[b] = jnp.sqrt(1.0 - a * a) * (ig * xc)

    def step(jj, hs):
        t = (ROW_TILE - 1 - jj) if reverse else jj
        out = []
        for b in range(nb):
            h = a_s[b, pl.ds(t, 1), :] * hs[b] + b_s[b, pl.ds(t, 1), :]
            o_ref[b, pl.ds(t, 1), :] = h
            out.append(h)
        return tuple(out)

    hs = lax.fori_loop(0, ROW_TILE, step, tuple(h_s[b] for b in range(nb)), unroll=8)
    for b in range(nb):
        h_s[b] = hs[b]


def _lru_call(cv3, wg, bg, sp, lc, d, reverse):
    nb, seq, _ = cv3.shape
    nt = seq // ROW_TILE
    nctx = lc // ROW_TILE

    def ch(i):
        return _chunk_order(i, nctx, nt, reverse)

    return pl.pallas_call(
        functools.partial(_lru_body, reverse=reverse, nb=nb),
        out_shape=jax.ShapeDtypeStruct((nb, seq, GW), f32),
        grid=(nt,),
        in_specs=[pl.BlockSpec((nb, ROW_TILE, GW), lambda i: (0, ch(i), 3)),
                  pl.BlockSpec((1, GW, 2 * GW), lambda i: (d, 0, 0)),
                  pl.BlockSpec((1, 1, 2 * GW), lambda i: (d, 0, 0)),
                  pl.BlockSpec((1, 1, GW), lambda i: (d, 0, 0))],
        out_specs=pl.BlockSpec((nb, ROW_TILE, GW), lambda i: (0, ch(i), 0)),
        scratch_shapes=[pltpu.VMEM((nb, ROW_TILE, GW), f32), pltpu.VMEM((nb, ROW_TILE, GW), f32),
                        pltpu.VMEM((nb, 1, GW), f32)],
        compiler_params=_cparams(("arbitrary",)),
        name="lru_rev" if reverse else "lru_fwd",
    )(cv3, wg, bg, sp)


def _tri_masks(reverse):
    ii = lax.broadcasted_iota(jnp.int32, (CHUNK, CHUNK), 0)
    jj = lax.broadcasted_iota(jnp.int32, (CHUNK, CHUNK), 1)
    if reverse:
        return jj >= ii, jj > ii, ii, jj
    return ii >= jj, ii > jj, ii, jj


def _rope(x, cos, sin):
    lane = lax.broadcasted_iota(jnp.int32, x.shape, 1)
    swapped = jnp.where(lane % 64 < 32, pltpu.roll(x, 96, 1), pltpu.roll(x, 32, 1))
    return x * cos + swapped * sin


def _each(fn, *lists):
    return [fn(*args) for args in zip(*lists)]


def _ret_body(lg_ref, q_ref, k_ref, v_ref, cos_ref, sin_ref, o_ref, s_s, *, reverse, nb):
    i = pl.program_id(0)

    @pl.when(i == 0)
    def _():
        s_s[...] = jnp.zeros_like(s_s)

    mask, _, ii, jj = _tri_masks(reverse)
    diff = ((jj - ii) if reverse else (ii - jj)).astype(f32)
    col = lax.broadcasted_iota(jnp.int32, (CHUNK, 1), 0).astype(f32)
    cos = cos_ref[...]
    sin = sin_ref[...]
    dec, eq, ek, ec = [], [], [], []
    for h in range(RET_HEADS):
        lg = lg_ref[h]
        dec.append(jnp.where(mask, jnp.exp(lg * diff), 0.0))
        eq.append(jnp.exp(lg * ((CHUNK - col) if reverse else (col + 1.0))))
        ek.append(jnp.exp(lg * (col if reverse else (CHUNK - 1.0 - col))))
        ec.append(jnp.exp(lg * CHUNK))

    def lanes(h):
        return slice(h * HEAD_DIM, (h + 1) * HEAD_DIM)

    chains = [(b, h) for b in range(nb) for h in range(RET_HEADS)]
    hd = [h for _, h in chains]
    qh = [_rope(q_ref[b, :, lanes(h)], cos, sin) * (HEAD_DIM ** -0.5) for b, h in chains]
    kh = [_rope(k_ref[b, :, lanes(h)], cos, sin) for b, h in chains]
    vh = [v_ref[b, :, lanes(h)] for b, h in chains]
    st = [s_s[b * RET_HEADS + h] for b, h in chains]
    sc = _each(lambda q, k, h: _dot_nt(q, k) * dec[h], qh, kh, hd)
    inter = _each(lambda q, s, h: _dot(q * eq[h], s), qh, st, hd)
    ys = _each(lambda s, v, y0: y0 + _dot(s, v), sc, vh, inter)
    new_s = _each(lambda s, k, v, h: ec[h] * s + _dot_tn(k * ek[h], v), st, kh, vh, hd)
    for (b, h), y, s in zip(chains, ys, new_s):
        o_ref[b, :, lanes(h)] = y.astype(o_ref.dtype)
        s_s[b * RET_HEADS + h] = s


def _ret_call(p3, lg, cos, sin, lc, reverse):
    nb, seq, _ = p3.shape
    nc = seq // CHUNK
    nctx = lc // CHUNK

    def ch(i):
        return _chunk_order(i, nctx, nc, reverse)

    def pspec(col):
        return pl.BlockSpec((nb, CHUNK, GW), lambda i: (0, ch(i), col // GW))

    return pl.pallas_call(
        functools.partial(_ret_body, reverse=reverse, nb=nb),
        out_shape=jax.ShapeDtypeStruct((nb, seq, GW), bf16),
        grid=(nc,),
        in_specs=[pl.BlockSpec(memory_space=pltpu.SMEM),
                  pspec(P_Q), pspec(P_K), pspec(P_V),
                  pl.BlockSpec((CHUNK, HEAD_DIM), lambda i: (ch(i), 0)),
                  pl.BlockSpec((CHUNK, HEAD_DIM), lambda i: (ch(i), 0))],
        out_specs=pl.BlockSpec((nb, CHUNK, GW), lambda i: (0, ch(i), 0)),
        scratch_shapes=[pltpu.VMEM((nb * RET_HEADS, HEAD_DIM, HEAD_DIM), f32)],
        compiler_params=_cparams(("arbitrary",)),
        name="ret_rev" if reverse else "ret_fwd",
    )(lg, p3, p3, p3, cos, sin)


def _cumsums(la_col, la_row, reverse):
    mask, _, ii, jj = _tri_masks(reverse)
    m_col = jnp.where(mask, 1.0, 0.0)
    m_row = jnp.where(jnp.logical_not(mask) | (ii == jj), 1.0, 0.0)
    cum_col = jnp.dot(m_col, la_col, precision=HIGHEST, preferred_element_type=f32)
    cum_row = jnp.dot(la_row, m_row, precision=HIGHEST, preferred_element_type=f32)
    return cum_col, cum_row, mask


def _ssd_body(xs_ref, bc_ref, sm_ref, smt_ref, pc_ref, pr_ref, o_ref, s_s, *, reverse, nb):
    i = pl.program_id(0)

    @pl.when(i == 0)
    def _():
        s_s[...] = jnp.zeros_like(s_s)

    lane = lax.broadcasted_iota(jnp.int32, (1, 128), 1)
    mask, _, _, _ = _tri_masks(reverse)
    prep = []
    for b in range(nb):
        dt_col = _softplus(sm_ref[b] + pc_ref[0, 0:1, :])
        la_col = pc_ref[0, 1:2, :] * dt_col
        la_row = pr_ref[0, :, 1:2] * _softplus(smt_ref[b, 0:SSD_HEADS, :] + pr_ref[0, :, 0:1])
        cum_col, cum_row, _ = _cumsums(la_col, la_row, reverse)
        tot = cum_col[0:1, :] if reverse else cum_col[CHUNK - 1:CHUNK, :]
        bm = bc_ref[b, :, :128]
        cm = bc_ref[b, :, 128:]
        grp = []
        for g in range(2):
            gmask = (lane // SSD_STATE) == g
            grp.append((jnp.where(gmask, cm, 0.0), jnp.where(gmask, bm, 0.0)))
        prep.append((dt_col, cum_col, cum_row, tot, grp))

    scores = {(b, g): _dot_nt(prep[b][4][g][0], prep[b][4][g][1]) for b in range(nb) for g in range(2)}
    heads = [(b, h) for b in range(nb) for h in range(SSD_HEADS)]

    def head_inputs(b, h):
        dt_col, cum_col, cum_row, tot, grp = prep[b]
        pp, hh, g = h // 2, h % 2, h // (SSD_HEADS // 2)
        hmask = (lane // SSD_HEAD_DIM) == hh
        cc = cum_col[:, h:h + 1]
        dec = jnp.where(mask, jnp.exp(cc - cum_row[h:h + 1, :]), 0.0)
        vh = jnp.where(hmask, xs_ref[b, :, pp * 128:(pp + 1) * 128] * dt_col[:, h:h + 1], 0.0)
        sth = jnp.where(hmask, s_s[b * 4 + pp], 0.0)
        return cc, dec, vh, sth, tot[:, h:h + 1], grp[g][0], grp[g][1], scores[(b, g)]

    ins = [head_inputs(b, h) for b, h in heads]
    intra = [_dot(sc * dec, vh) for _, dec, vh, _, _, _, _, sc in ins]
    inter = [_dot(cmg * jnp.exp(cc), sth) for cc, _, _, sth, _, cmg, _, _ in ins]
    upd = [jnp.exp(th) * sth + _dot_tn(bmg * jnp.exp(th - cc), vh) for cc, _, vh, sth, th, _, bmg, _ in ins]
    for j in range(0, len(heads), 2):
        b, h = heads[j]
        pp = h // 2
        o_ref[b, :, pp * 128:(pp + 1) * 128] = (
            (intra[j] + inter[j]) + (intra[j + 1] + inter[j + 1])).astype(o_ref.dtype)
        s_s[b * 4 + pp] = upd[j] + upd[j + 1]


def _ssd_call(cv3, p3, smt3, pc, pr, lc, d, reverse):
    nb, seq, _ = cv3.shape
    nc = seq // CHUNK
    nctx = lc // CHUNK

    def ch(i):
        return _chunk_order(i, nctx, nc, reverse)

    return pl.pallas_call(
        functools.partial(_ssd_body, reverse=reverse, nb=nb),
        out_shape=jax.ShapeDtypeStruct((nb, seq, GW), bf16),
        grid=(nc,),
        in_specs=[pl.BlockSpec((nb, CHUNK, GW), lambda i: (0, ch(i), 4)),
                  pl.BlockSpec((nb, CHUNK, 256), lambda i: (0, ch(i), 10)),
                  pl.BlockSpec((nb, CHUNK, 128), lambda i: (0, ch(i), P_SMALL // 128)),
                  pl.BlockSpec((nb, 32, CHUNK), lambda i: (0, 0, ch(i))),
                  pl.BlockSpec((1, 2, 128), lambda i: (d, 0, 0)),
                  pl.BlockSpec((1, 8, 2), lambda i: (d, 0, 0))],
        out_specs=pl.BlockSpec((nb, CHUNK, GW), lambda i: (0, ch(i), 0)),
        scratch_shapes=[pltpu.VMEM((nb * 4, 128, 128), f32)],
        compiler_params=_cparams(("arbitrary",)),
        name="ssd_rev" if reverse else "ssd_fwd",
    )(cv3, cv3, p3, smt3, pc, pr)


TRI_BASE = 8
GDN_GROUP = 16


def _unit_tri_inverse_minus_eye(ms, ii, jj):
    base = (ii // TRI_BASE) == (jj // TRI_BASE)
    pws = _each(lambda m: jnp.where(base, -m, 0.0), ms)
    accs = pws
    size = 2
    while size < TRI_BASE:
        pws = _each(lambda p: _dot(p, p), pws)
        accs = _each(lambda a, p: a + p + _dot(a, p), accs, pws)
        size *= 2
    half = TRI_BASE
    while half < CHUNK:
        off = jnp.logical_and((ii // (2 * half)) == (jj // (2 * half)), (ii // half) != (jj // half))
        mos = _each(lambda m: jnp.where(off, m, 0.0), ms)
        xs = _each(lambda a, mo: mo + _dot(a, mo), accs, mos)
        accs = _each(lambda a, x: a - x - _dot(x, a), accs, xs)
        half *= 2
    return accs


def _gdn_body(q_ref, k_ref, v_ref, sm_ref, smt_ref, pc_ref, pr_ref, o_ref, s_s, *, reverse, d, nb, group):
    i = pl.program_id(0)

    @pl.when(i == 0)
    def _():
        s_s[...] = jnp.zeros_like(s_s)

    mask, strict, blk_i, blk_j = _tri_masks(reverse)
    per_b = []
    for b in range(nb):
        sm = sm_ref[b]
        la_col = pc_ref[0, 1:2, :] * _softplus(sm + pc_ref[0, 0:1, :])
        la_row = pr_ref[0, :, 1:2] * _softplus(smt_ref[b] + pr_ref[0, :, 0:1])
        cum_col, cum_row, _ = _cumsums(la_col, la_row, reverse)
        tot = cum_col[0:1, :] if reverse else cum_col[CHUNK - 1:CHUNK, :]
        per_b.append((cum_col, cum_row, tot, _sigmoid(sm)))

    chains = [(b, h) for b in range(nb) for h in range(GDN_HEADS)]
    for g0 in range(0, len(chains), group):
        grp = chains[g0:g0 + group]

        def lanes(h):
            return slice(h * HEAD_DIM, (h + 1) * HEAD_DIM)

        def l2n(x):
            return x * lax.rsqrt(jnp.sum(x * x, axis=-1, keepdims=True) + 1e-6)

        ca = [8 + d * GDN_HEADS + h for _, h in grp]
        cb = [16 + d * GDN_HEADS + h for _, h in grp]
        qn = [l2n(q_ref[b, :, lanes(h)]) * (HEAD_DIM ** -0.5) for b, h in grp]
        kn = [l2n(k_ref[b, :, lanes(h)]) for b, h in grp]
        vh = [v_ref[b, :, lanes(h)] for b, h in grp]
        cc = [per_b[b][0][:, c:c + 1] for (b, _), c in zip(grp, ca)]
        cr = [per_b[b][1][c:c + 1, :] for (b, _), c in zip(grp, ca)]
        th = [per_b[b][2][:, c:c + 1] for (b, _), c in zip(grp, ca)]
        beta = [per_b[b][3][:, c:c + 1] for (b, _), c in zip(grp, cb)]
        st = [s_s[b * GDN_HEADS + h] for b, h in grp]
        dec = _each(lambda c, r: jnp.where(mask, jnp.exp(c - r), 0.0), cc, cr)
        kb = _each(lambda k, bt: k * bt, kn, beta)
        ms = _each(lambda a, k, dc: jnp.where(strict, _dot_nt(a, k) * dc, 0.0), kb, kn, dec)
        attn = _each(lambda q, k, dc: _dot_nt(q, k) * dc, qn, kn, dec)
        qs = _each(lambda q, c, s: _dot(q * jnp.exp(c), s), qn, cc, st)
        accs = _unit_tri_inverse_minus_eye(ms, blk_i, blk_j)
        rhs = _each(lambda v, bt, k, c: jnp.concatenate([v * bt, k * jnp.exp(c)], axis=1), vh, beta, kb, cc)
        sol = _each(lambda r, a: r + _dot(a, r), rhs, accs)
        v_new = _each(lambda s_, s: s_[:, :HEAD_DIM] - _dot(s_[:, HEAD_DIM:], s), sol, st)
        outs = _each(lambda o, a, v: o + _dot(a, v), qs, attn, v_new)
        new_s = _each(lambda t, s, k, c, v: jnp.exp(t) * s + _dot_tn(k * jnp.exp(t - c), v), th, st, kn, cc, v_new)
        for (b, h), o, s in zip(grp, outs, new_s):
            o_ref[b, :, lanes(h)] = o.astype(o_ref.dtype)
            s_s[b * GDN_HEADS + h] = s


def _gdn_call(cv3, p3, smt3, pc, pr, lc, d, reverse):
    nb, seq, _ = cv3.shape
    nc = seq // CHUNK
    nctx = lc // CHUNK

    def ch(i):
        return _chunk_order(i, nctx, nc, reverse)

    def cspec(col):
        return pl.BlockSpec((nb, CHUNK, GW), lambda i: (0, ch(i), col))

    return pl.pallas_call(
        functools.partial(_gdn_body, reverse=reverse, d=d, nb=nb, group=GDN_GROUP),
        out_shape=jax.ShapeDtypeStruct((nb, seq, GW), bf16),
        grid=(nc,),
        in_specs=[cspec(0), cspec(1), cspec(2),
                  pl.BlockSpec((nb, CHUNK, 128), lambda i: (0, ch(i), P_SMALL // 128)),
                  pl.BlockSpec((nb, 32, CHUNK), lambda i: (0, 0, ch(i))),
                  pl.BlockSpec((1, 2, 128), lambda i: (d, 0, 0)),
                  pl.BlockSpec((1, 32, 2), lambda i: (d, 0, 0))],
        out_specs=pl.BlockSpec((nb, CHUNK, GW), lambda i: (0, ch(i), 0)),
        scratch_shapes=[pltpu.VMEM((nb * GDN_HEADS, HEAD_DIM, HEAD_DIM), f32)],
        compiler_params=_cparams(("arbitrary",)),
        name="gdn_rev" if reverse else "gdn_fwd",
    )(cv3, cv3, cv3, p3, smt3, pc, pr)


def _rms_lanes(x, w, eps):
    return x * lax.rsqrt(jnp.mean(x * x, axis=-1, keepdims=True) + eps) * w


def _finish_body(hf, hb, rf, rb, sf, sb, gf, gb, gate, rg, cz, dz, xs, retw, ssdd, ssdw, gdnw, o_ref):
    g = gate[...]
    gelu = 0.5 * g * (1.0 + jnp.tanh(math.sqrt(2.0 / math.pi) * (g + 0.044715 * (g * g * g))))
    o_ref[:, 0:GW] = (gelu * (hf[...] + hb[...])).astype(bf16)

    ro = rf[...].astype(f32) + rb[...].astype(f32)
    rgate = _silu(rg[...])
    go = gf[...].astype(f32) + gb[...].astype(f32)
    ggate = _silu(dz[...])
    for h in range(RET_HEADS):
        lanes = slice(h * HEAD_DIM, (h + 1) * HEAD_DIM)
        o_ref[:, GW + h * HEAD_DIM:GW + (h + 1) * HEAD_DIM] = (
            rgate[:, lanes] * _rms_lanes(ro[:, lanes], retw[:, lanes], 1e-6)).astype(bf16)
        o_ref[:, 3 * GW + h * HEAD_DIM:3 * GW + (h + 1) * HEAD_DIM] = (
            _rms_lanes(go[:, lanes], gdnw[:, lanes], 1e-6) * ggate[:, lanes]).astype(bf16)

    sy = (sf[...].astype(f32) + sb[...].astype(f32) + ssdd[...] * xs[...]) * _silu(cz[...])
    o_ref[:, 2 * GW:3 * GW] = _rms_lanes(sy, ssdw[...], 1e-6).astype(bf16)


def _finish_call(scans, p, cv, retw, ssdd, ssdw, gdnw):
    n = p.shape[0]
    tm = ROW_TILE

    def rowspec(col):
        return pl.BlockSpec((tm, GW), lambda i: (i, col))

    vec = pl.BlockSpec((1, GW), lambda i: (0, 0))
    return pl.pallas_call(
        _finish_body,
        out_shape=jax.ShapeDtypeStruct((n, D_MODEL), bf16),
        grid=(n // tm,),
        in_specs=[rowspec(0)] * 8 + [rowspec(P_GATE // GW), rowspec(P_G // GW), rowspec(P_CZ // GW),
                                     rowspec(P_DZ // GW), rowspec(4), vec, vec, vec, vec],
        out_specs=pl.BlockSpec((tm, D_MODEL), lambda i: (i, 0)),
        compiler_params=_cparams(("arbitrary",)),
        name="mixer_finish",
    )(*scans, p, p, p, p, cv, retw, ssdd, ssdw, gdnw)


def _layer_norm_rows(t, w, b):
    mu = jnp.mean(t, axis=-1, keepdims=True)
    tc = t - mu
    var = jnp.mean(tc * tc, axis=-1, keepdims=True)
    return tc * lax.rsqrt(var + 1e-5) * w + b


HALF = D_MODEL // 2
HI_MASK = 0xFFFF0000


def _pack_bf16_pair(lo, hi):
    lo_bits = lax.bitcast_convert_type(lo.astype(bf16).astype(f32), jnp.uint32)
    hi_bits = lax.bitcast_convert_type(hi.astype(bf16).astype(f32), jnp.uint32)
    return hi_bits | (lo_bits >> 16)


def _unpack_bf16_pair(w):
    lo = lax.bitcast_convert_type(w << 16, f32)
    hi = lax.bitcast_convert_type(w & jnp.uint32(HI_MASK), f32)
    return lo, hi


TOKEN_ROWS = HALF // 128


def _store_token_tiles(ref, packed):
    m = packed.shape[0]
    for c in range(TOKEN_ROWS):
        ref[pl.ds(c, m, stride=TOKEN_ROWS), :] = packed[:, c * 128:(c + 1) * 128]


def _load_token_tiles(ref, m):
    return jnp.concatenate([ref[pl.ds(c, m, stride=TOKEN_ROWS), :] for c in range(TOKEN_ROWS)], axis=1)


def _split_bf16(x):
    hi = x.astype(bf16)
    return hi, (x - hi.astype(f32)).astype(bf16)


def _outproj_body(y_ref, w_ref, x_ref, g1_ref, lw_ref, lb_ref, sc_ref, sh_ref, rwh_ref, rwl_ref,
                  x1_ref, u_ref, lg_ref, *, alpha):
    y = jnp.dot(y_ref[...], w_ref[...], preferred_element_type=f32)
    x1 = _layer_norm_rows(alpha * x_ref[...] + g1_ref[0] * y, lw_ref[...], lb_ref[...])
    x1_ref[...] = x1
    u = x1 * (1.0 + sc_ref[0]) + sh_ref[0]
    _store_token_tiles(u_ref, _pack_bf16_pair(u[:, :HALF], u[:, HALF:]))
    uh, ul = _split_bf16(u)
    rwh = rwh_ref[...]
    lg = (jnp.dot(uh, rwh, preferred_element_type=f32) + jnp.dot(ul, rwh, preferred_element_type=f32)
          + jnp.dot(uh, rwl_ref[...], preferred_element_type=f32))
    lg_ref[...] = lg.T[:N_EXPERTS, :]


def _outproj_call(ycat, w_bf, x, g1, lw, lb, sc2, sh2, rw_hi, rw_lo, alpha):
    n, d = x.shape
    tm = ROW_TILE
    mod = pl.BlockSpec((1, 1, d), lambda i: (i, 0, 0))
    vec = pl.BlockSpec((1, d), lambda i: (0, 0))
    rws = pl.BlockSpec((d, 2 * N_EXPERTS), lambda i: (0, 0))
    return pl.pallas_call(
        functools.partial(_outproj_body, alpha=alpha),
        out_shape=(jax.ShapeDtypeStruct((n, d), f32), jax.ShapeDtypeStruct((n * TOKEN_ROWS, 128), jnp.uint32),
                   jax.ShapeDtypeStruct((N_EXPERTS, n), f32)),
        grid=(n // tm,),
        in_specs=[pl.BlockSpec((tm, d), lambda i: (i, 0)),
                  pl.BlockSpec((d, d), lambda i: (0, 0)),
                  pl.BlockSpec((tm, d), lambda i: (i, 0)),
                  mod, vec, vec, mod, mod, rws, rws],
        out_specs=(pl.BlockSpec((tm, d), lambda i: (i, 0)), pl.BlockSpec((tm * TOKEN_ROWS, 128), lambda i: (i, 0)),
                   pl.BlockSpec((N_EXPERTS, tm), lambda i: (0, i))),
        compiler_params=_cparams(("arbitrary",)),
        name="outproj_ln1",
    )(ycat, w_bf, x, g1, lw, lb, sc2, sh2, rw_hi, rw_lo)


EXPERT_TILE = 512


def _num_expert_tiles(n_pairs):
    return (n_pairs + N_EXPERTS * (EXPERT_TILE - 1) + EXPERT_TILE - 1) // EXPERT_TILE


def _route_rank_body(lg_ref, bias_ref, idx_ref, wt_ref, rank_ref, cnt_ref, carry_s):
    i = pl.program_id(0)

    @pl.when(i == 0)
    def _():
        carry_s[...] = jnp.zeros_like(carry_s)

    tn = lg_ref.shape[1]
    scores = _sigmoid(lg_ref[...])
    sel = scores + bias_ref[...]
    gsz = N_EXPERTS // N_EXPERT_GROUPS
    neg = -jnp.inf
    sel3 = sel.reshape(N_EXPERT_GROUPS, gsz, tn)
    io3 = lax.broadcasted_iota(jnp.int32, sel3.shape, 1)
    m1 = jnp.max(sel3, axis=1)
    first = jnp.min(jnp.where(sel3 == m1[:, None, :], io3, gsz), axis=1)
    m2 = jnp.max(jnp.where(io3 == first[:, None, :], neg, sel3), axis=1)
    gscore = m1 + m2
    iog = lax.broadcasted_iota(jnp.int32, gscore.shape, 0)
    gsel = None
    for _ in range(TOPK_GROUPS):
        gm = jnp.max(gscore, axis=0, keepdims=True)
        gi = jnp.min(jnp.where(gscore == gm, iog, N_EXPERT_GROUPS), axis=0, keepdims=True)
        hit = iog == gi
        gsel = hit if gsel is None else jnp.logical_or(gsel, hit)
        gscore = jnp.where(hit, neg, gscore)
    emask = jnp.broadcast_to(gsel[:, None, :], sel3.shape).reshape(N_EXPERTS, tn)
    cand = jnp.where(emask, sel, neg)
    ioe = lax.broadcasted_iota(jnp.int32, cand.shape, 0)
    idxs = []
    wts = []
    hits = []
    for _ in range(TOP_K):
        cm = jnp.max(cand, axis=0, keepdims=True)
        ci = jnp.min(jnp.where(cand == cm, ioe, N_EXPERTS), axis=0, keepdims=True)
        hit = ioe == ci
        idxs.append(ci)
        hits.append(hit)
        wts.append(jnp.sum(jnp.where(hit, scores, 0.0), axis=0, keepdims=True))
        cand = jnp.where(hit, neg, cand)
    wsum = wts[0]
    for w in wts[1:]:
        wsum = wsum + w
    idx_ref[...] = jnp.concatenate(idxs, axis=0)
    wnorm = jnp.concatenate([w / wsum * ROUTED_SCALE for w in wts], axis=0)
    wt_ref[...] = jnp.concatenate([wnorm, jnp.zeros((128 - TOP_K, tn), f32)], axis=0).T

    r_io = lax.broadcasted_iota(jnp.int32, (tn, tn), 0)
    c_io = lax.broadcasted_iota(jnp.int32, (tn, tn), 1)
    prefix = jnp.where(r_io <= c_io, 1.0, 0.0).astype(bf16)
    base = carry_s[...]
    ranks = []
    for hit in hits:
        onehot = jnp.where(hit, 1.0, 0.0)
        cum = jnp.dot(onehot.astype(bf16), prefix, preferred_element_type=f32)
        ranks.append(jnp.sum(onehot * (cum - 1.0 + base), axis=0, keepdims=True))
        base = base + cum[:, tn - 1:tn]
    carry_s[...] = base
    rank_ref[...] = jnp.concatenate(ranks, axis=0).astype(jnp.int32)
    cnt_ref[...] = jnp.broadcast_to(base, cnt_ref.shape)


def _route_rank_call(logits_t, bias):
    n = logits_t.shape[1]
    tn = ROW_TILE
    kt = pl.BlockSpec((TOP_K, tn), lambda i: (0, i))
    return pl.pallas_call(
        _route_rank_body,
        out_shape=(jax.ShapeDtypeStruct((TOP_K, n), jnp.int32),
                   jax.ShapeDtypeStruct((n, 128), f32),
                   jax.ShapeDtypeStruct((TOP_K, n), jnp.int32),
                   jax.ShapeDtypeStruct((N_EXPERTS, 128), f32)),
        grid=(n // tn,),
        in_specs=[pl.BlockSpec((N_EXPERTS, tn), lambda i: (0, i)),
                  pl.BlockSpec((N_EXPERTS, 1), lambda i: (0, 0))],
        out_specs=(kt, pl.BlockSpec((tn, 128), lambda i: (i, 0)), kt,
                   pl.BlockSpec((N_EXPERTS, 128), lambda i: (0, 0))),
        scratch_shapes=[pltpu.VMEM((N_EXPERTS, 1), f32)],
        compiler_params=_cparams(("arbitrary",)),
        name="router_rank",
    )(logits_t, bias)


def _expert_layout(counts, nt):
    tiles_e = (counts + EXPERT_TILE - 1) // EXPERT_TILE
    tend = jnp.cumsum(tiles_e)
    tstart = tend - tiles_e
    n_used = tend[-1:].astype(jnp.int32)
    tile = jnp.arange(nt, dtype=jnp.int32)
    te = jnp.minimum(jnp.sum((tend[None, :] <= tile[:, None]).astype(jnp.int32), axis=1), N_EXPERTS - 1)
    last_tile = jnp.where(tiles_e > 0, tend - 1, -1).astype(jnp.int32)
    pstart = (tstart * EXPERT_TILE).astype(f32)[:, None]
    return te.astype(jnp.int32), n_used, last_tile, pstart


def _dispatch_body(last_ref, nu_ref, u_ref, idx_ref, rank_ref, pstart_ref, pos_ref, xs_hbm,
                   zbuf, pos_s, zsem, psem, ssem):
    i = pl.program_id(0)
    tn = idx_ref.shape[1]
    tr = TOKEN_ROWS
    tile_rows = EXPERT_TILE * tr
    n_tiles = xs_hbm.shape[0] // tile_rows

    @pl.when(i == 0)
    def _():
        zbuf[...] = jnp.zeros_like(zbuf)

        def zero_tile(t):
            row = pl.multiple_of(t * tile_rows, tile_rows)
            return pltpu.make_async_copy(zbuf, xs_hbm.at[pl.ds(row, tile_rows)], zsem)

        for e in range(N_EXPERTS):
            @pl.when(last_ref[e] >= 0)
            def _(e=e):
                zero_tile(last_ref[e]).start()
        lax.fori_loop(nu_ref[0], n_tiles, lambda t, c: (zero_tile(t).start(), c)[1], 0)
        for e in range(N_EXPERTS):
            @pl.when(last_ref[e] >= 0)
            def _():
                zero_tile(0).wait()
        lax.fori_loop(nu_ref[0], n_tiles, lambda t, c: (zero_tile(0).wait(), c)[1], 0)

    ioe = lax.broadcasted_iota(jnp.int32, (N_EXPERTS, tn), 0)
    pstart = pstart_ref[...]
    pos = []
    for k in range(TOP_K):
        start_k = jnp.sum(jnp.where(ioe == idx_ref[k:k + 1, :], pstart, 0.0), axis=0, keepdims=True)
        pos.append(start_k.astype(jnp.int32) + rank_ref[k:k + 1, :])
    pos_ref[...] = jnp.concatenate(pos, axis=0)
    to_smem = pltpu.make_async_copy(pos_ref, pos_s, psem)
    to_smem.start()
    to_smem.wait()

    def per_token(t, carry):
        src = u_ref.at[pl.ds(pl.multiple_of(t * tr, tr), tr)]
        for k in range(TOP_K):
            row = pl.multiple_of(pos_s[k, t] * tr, tr)
            pltpu.make_async_copy(src, xs_hbm.at[pl.ds(row, tr)], ssem).start(priority=k % 2)
        return carry

    lax.fori_loop(0, tn, per_token, 0, unroll=4)
    for k in range(TOP_K):
        pltpu.make_async_copy(u_ref, xs_hbm.at[pl.ds(0, tn * tr)], ssem).wait()


def _dispatch_call(last_tile, n_used, u_pk, idx, rank, pstart, nt):
    n = idx.shape[1]
    tn = ROW_TILE
    tr = TOKEN_ROWS
    kt = pl.BlockSpec((TOP_K, tn), lambda i, last, nu: (0, i))
    grid_spec = pltpu.PrefetchScalarGridSpec(
        num_scalar_prefetch=2,
        grid=(n // tn,),
        in_specs=[pl.BlockSpec((tn * tr, 128), lambda i, last, nu: (i, 0)), kt, kt,
                  pl.BlockSpec((N_EXPERTS, 1), lambda i, last, nu: (0, 0))],
        out_specs=(kt, pl.BlockSpec(memory_space=pl.ANY)),
        scratch_shapes=[pltpu.VMEM((EXPERT_TILE * tr, 128), jnp.uint32),
                        pltpu.SMEM((TOP_K, tn), jnp.int32),
                        pltpu.SemaphoreType.DMA(()), pltpu.SemaphoreType.DMA(()), pltpu.SemaphoreType.DMA(())])
    return pl.pallas_call(
        _dispatch_body,
        out_shape=(jax.ShapeDtypeStruct((TOP_K, n), jnp.int32),
                   jax.ShapeDtypeStruct((nt * EXPERT_TILE * tr, 128), jnp.uint32)),
        grid_spec=grid_spec,
        compiler_params=_cparams(("arbitrary",)),
        name="dispatch",
    )(last_tile, n_used, u_pk, idx, rank, pstart)


def _sorted_experts_body(te_ref, nu_ref, x_ref, w1_ref, w3_ref, w2_ref, o_ref, w1_s, w3_s, w2_s):
    i = pl.program_id(0)
    nu = nu_ref[0]
    tm = EXPERT_TILE

    @pl.when(i < nu)
    def _():
        prev = te_ref[jnp.maximum(i - 1, 0)]

        @pl.when(jnp.logical_or(i == 0, te_ref[i] != prev))
        def _():
            w1_s[...] = w1_ref[0, 0].astype(bf16)
            w3_s[...] = w3_ref[0, 0].astype(bf16)
            w2_s[...] = w2_ref[0, 0].astype(bf16)

        lo, hi = _unpack_bf16_pair(_load_token_tiles(x_ref, tm))
        lo = lo.astype(bf16)
        hi = hi.astype(bf16)
        h1 = (jnp.dot(lo, w1_s[:HALF, :], preferred_element_type=f32)
              + jnp.dot(hi, w1_s[HALF:, :], preferred_element_type=f32))
        h3 = (jnp.dot(lo, w3_s[:HALF, :], preferred_element_type=f32)
              + jnp.dot(hi, w3_s[HALF:, :], preferred_element_type=f32))
        y = jnp.dot((_silu(h1) * h3).astype(bf16), w2_s[...], preferred_element_type=f32)
        _store_token_tiles(o_ref, _pack_bf16_pair(y[:, :HALF], y[:, HALF:]))

    @pl.when(i >= nu)
    def _():
        o_ref[...] = jnp.zeros_like(o_ref)


def _sorted_experts_call(te, n_used, xs, w1, w3, w2, layer):
    nt = te.shape[0]
    _, ne, d, fe = w1.shape
    tm = EXPERT_TILE
    rows = tm * TOKEN_ROWS

    def wmap(i, te_r, nu_r):
        return (layer, te_r[jnp.minimum(i, nu_r[0] - 1)], 0, 0)

    grid_spec = pltpu.PrefetchScalarGridSpec(
        num_scalar_prefetch=2,
        grid=(nt,),
        in_specs=[pl.BlockSpec((rows, 128), lambda i, te_r, nu_r: (jnp.minimum(i, nu_r[0] - 1), 0)),
                  pl.BlockSpec((1, 1, d, fe), wmap), pl.BlockSpec((1, 1, d, fe), wmap),
                  pl.BlockSpec((1, 1, fe, d), wmap)],
        out_specs=pl.BlockSpec((rows, 128), lambda i, te_r, nu_r: (i, 0)),
        scratch_shapes=[pltpu.VMEM((d, fe), bf16), pltpu.VMEM((d, fe), bf16), pltpu.VMEM((fe, d), bf16)])
    return pl.pallas_call(
        _sorted_experts_body,
        out_shape=jax.ShapeDtypeStruct(xs.shape, jnp.uint32),
        grid_spec=grid_spec,
        compiler_params=_cparams(("arbitrary",)),
        name="experts_sorted",
    )(te, n_used, xs, w1, w3, w2)


def _gather_combine_body(pos_ref, posn_ref, wt_ref, u_ref, x_ref, g2_ref, lw_ref, lb_ref, s1_ref, s3_ref, s2_ref,
                         ys_hbm, o_ref, pos_s, gbuf, psem, gsem, *, alpha):
    i = pl.program_id(0)
    tn = x_ref.shape[0]
    tr = TOKEN_ROWS
    slot = lax.rem(i, 2)
    nslot = 1 - slot

    def start_gathers(src_pos_ref, s):
        to_smem = pltpu.make_async_copy(src_pos_ref, pos_s.at[s], psem)
        to_smem.start()
        to_smem.wait()
        for t in range(tn):
            for k in range(TOP_K):
                row = pl.multiple_of(pos_s[s, k, t] * tr, tr)
                pltpu.make_async_copy(ys_hbm.at[pl.ds(row, tr)], gbuf.at[s, k, pl.ds(t * tr, tr)],
                                      gsem.at[s]).start(priority=k % 2)

    @pl.when(i == 0)
    def _():
        start_gathers(pos_ref, 0)

    @pl.when(i + 1 < pl.num_programs(0))
    def _():
        start_gathers(posn_ref, nslot)

    ulo, uhi = _unpack_bf16_pair(_load_token_tiles(u_ref, tn))
    ulo = ulo.astype(bf16)
    uhi = uhi.astype(bf16)
    a1 = (jnp.dot(ulo, s1_ref[:HALF, :], preferred_element_type=f32)
          + jnp.dot(uhi, s1_ref[HALF:, :], preferred_element_type=f32))
    a3 = (jnp.dot(ulo, s3_ref[:HALF, :], preferred_element_type=f32)
          + jnp.dot(uhi, s3_ref[HALF:, :], preferred_element_type=f32))
    shared = jnp.dot((_silu(a1) * a3).astype(bf16), s2_ref[...], preferred_element_type=f32)

    for k in range(TOP_K):
        pltpu.make_async_copy(ys_hbm.at[pl.ds(0, tn * tr)], gbuf.at[slot, k], gsem.at[slot]).wait()
    wt = wt_ref[...]
    lo = None
    hi = None
    for k in range(TOP_K):
        lo_k, hi_k = _unpack_bf16_pair(_load_token_tiles(gbuf.at[slot, k], tn))
        gate = wt[:, k:k + 1]
        lo = gate * lo_k if lo is None else lo + gate * lo_k
        hi = gate * hi_k if hi is None else hi + gate * hi_k
    f = jnp.concatenate([lo, hi], axis=1) + shared
    o_ref[...] = _layer_norm_rows(alpha * x_ref[...] + g2_ref[0] * f, lw_ref[...], lb_ref[...])


def _gather_combine_call(pos, wt_tm, u_pk, x1, g2, lw, lb, s1, s3, s2, ys, alpha):
    n, d = x1.shape
    tn = ROW_TILE
    tr = TOKEN_ROWS
    fe = s1.shape[1]
    row = pl.BlockSpec((tn, d), lambda i: (i, 0))
    vec = pl.BlockSpec((1, d), lambda i: (0, 0))
    last = n // tn - 1
    return pl.pallas_call(
        functools.partial(_gather_combine_body, alpha=alpha),
        out_shape=jax.ShapeDtypeStruct((n, d), f32),
        grid=(n // tn,),
        in_specs=[pl.BlockSpec((TOP_K, tn), lambda i: (0, i)),
                  pl.BlockSpec((TOP_K, tn), lambda i: (0, jnp.minimum(i + 1, last))),
                  pl.BlockSpec((tn, 128), lambda i: (i, 0)),
                  pl.BlockSpec((tn * tr, 128), lambda i: (i, 0)), row,
                  pl.BlockSpec((1, 1, d), lambda i: (i, 0, 0)), vec, vec,
                  pl.BlockSpec((d, fe), lambda i: (0, 0)), pl.BlockSpec((d, fe), lambda i: (0, 0)),
                  pl.BlockSpec((fe, d), lambda i: (0, 0)),
                  pl.BlockSpec(memory_space=pl.ANY)],
        out_specs=row,
        scratch_shapes=[pltpu.SMEM((2, TOP_K, tn), jnp.int32),
                        pltpu.VMEM((2, TOP_K, tn * tr, 128), jnp.uint32),
                        pltpu.SemaphoreType.DMA(()), pltpu.SemaphoreType.DMA((2,))],
        compiler_params=_cparams(("arbitrary",)),
        name="gather_combine_ln2",
    )(pos, pos, wt_tm, u_pk, x1, g2, lw, lb, s1, s3, s2, ys)


def _block_diag(w):
    g, n, _ = w.shape
    eye = jnp.eye(g, dtype=w.dtype)
    return (eye[:, None, :, None] * w[:, :, None, :]).reshape(g * n, g * n)


def _rope_tables(seq, lc):
    half = HEAD_DIM // 2
    t = jnp.arange(seq - lc, dtype=f32)
    inv = ROPE_BASE ** (-jnp.arange(0, half, 2, dtype=f32) / half)
    ar = jnp.floor(t / GRID_W)[:, None] * inv[None, :]
    ac = (t - jnp.floor(t / GRID_W) * GRID_W)[:, None] * inv[None, :]
    cos = jnp.concatenate([jnp.cos(ar), jnp.cos(ar), jnp.cos(ac), jnp.cos(ac)], axis=1)
    sin = jnp.concatenate([-jnp.sin(ar), jnp.sin(ar), -jnp.sin(ac), jnp.sin(ac)], axis=1)
    cos = jnp.concatenate([jnp.ones((lc, HEAD_DIM), f32), cos], axis=0)
    sin = jnp.concatenate([jnp.zeros((lc, HEAD_DIM), f32), sin], axis=0)
    return cos, sin


def _pad_lanes(v, offset, width=128):
    return jnp.zeros((width,), f32).at[offset:offset + v.shape[0]].set(v)


def kernel(x, c, ctx, c_ctx, w_ada, b_ada, w_in, lru_conv_w, lru_conv_b, lru_gate_a_w, lru_gate_a_b,
           lru_gate_x_w, lru_gate_x_b, lru_lambda, ret_log_decay, ret_norm_w, ssd_conv_w, ssd_conv_b,
           ssd_dt_bias, ssd_a_log, ssd_d, ssd_norm_w, gdn_conv_w, gdn_dt_bias, gdn_a_log, gdn_norm_w,
           w_out, ln1_w, ln1_b, router_w, router_bias, exp_w1, exp_w3, exp_w2, sh_w1, sh_w3, sh_w2,
           ln2_w, ln2_b):
    nb, lat, d = x.shape
    lc = ctx.shape[1]
    depth = w_ada.shape[0]
    seq = lc + lat
    n = nb * seq
    ntile = n // ROW_TILE
    tiles_per_seq = seq // ROW_TILE
    alpha = (2 * depth) ** 0.25
    assert d == D_MODEL and lc == ROW_TILE and lat % ROW_TILE == 0 and nb <= 7

    xall = jnp.concatenate([ctx, x], axis=1).reshape(n, d)

    svec = jnp.zeros((8, d), f32).at[0].set(c_ctx).at[1:1 + nb].set(c)
    modtab = _ada_call(svec, w_ada, b_ada).reshape(depth, 8, 6, d)
    tile = jnp.arange(ntile)
    tile_row = jnp.where(tile % tiles_per_seq == 0, 0, 1 + tile // tiles_per_seq)
    cos, sin = _rope_tables(seq, lc)

    for l in range(depth):
        mods = [modtab[l, :, k, :][tile_row][:, None, :] for k in range(6)]
        sh1, sc1, g1, sh2, sc2, g2 = mods

        wl = w_in[l]
        w_re = jnp.concatenate([wl[:, :3584], wl[:, 5896:6408], wl[:, 4360:5896], wl[:, 3584:4352],
                                wl[:, 4352:4360], wl[:, 6408:6424],
                                jnp.zeros((d, IN_COLS_PAD - 6424), f32)], axis=1).astype(bf16)
        p = _inproj_call(xall, sc1, sh1, w_re)

        wcat = jnp.concatenate([gdn_conv_w[l], lru_conv_w[l], ssd_conv_w[l]], axis=1)
        bcat = jnp.concatenate([jnp.zeros((3 * GW,), f32), lru_conv_b[l], ssd_conv_b[l]])[None, :]
        cv = _conv_call(p, wcat, bcat, nb, seq, lc)
        p3 = p.reshape(nb, seq, IN_COLS_PAD)
        cv3 = cv.reshape(nb, seq, CV_COLS)
        smt3 = jnp.transpose(p3[:, :, P_SMALL:P_SMALL + 32], (0, 2, 1))

        wg = jnp.stack([jnp.concatenate([_block_diag(lru_gate_a_w[l, dd]), _block_diag(lru_gate_x_w[l, dd])],
                                        axis=1) for dd in range(2)]).astype(bf16)
        bg = jnp.concatenate([lru_gate_a_b[l], lru_gate_x_b[l]], axis=1)[:, None, :]
        sp = jax.nn.softplus(-lru_lambda[l])[:, None, :]
        hs = [_lru_call(cv3, wg, bg, sp, lc, dd, bool(dd)).reshape(n, GW) for dd in range(2)]

        lg = -jnp.exp(ret_log_decay[l].astype(f32))
        rs = [_ret_call(p3, lg[dd], cos, sin, lc, bool(dd)).reshape(n, GW) for dd in range(2)]

        a_ssd = -jnp.exp(ssd_a_log[l])
        pc = jnp.stack([jnp.stack([_pad_lanes(ssd_dt_bias[l, dd], 0), _pad_lanes(a_ssd[dd], 0)])
                        for dd in range(2)])
        pr = jnp.stack([jnp.stack([ssd_dt_bias[l, dd], a_ssd[dd]], axis=1) for dd in range(2)])
        ss = [_ssd_call(cv3, p3, smt3, pc, pr, lc, dd, bool(dd)).reshape(n, GW) for dd in range(2)]

        a_gdn = -jnp.exp(gdn_a_log[l])
        pcg = jnp.stack([jnp.stack([_pad_lanes(gdn_dt_bias[l, dd], 8 + 4 * dd),
                                    _pad_lanes(a_gdn[dd], 8 + 4 * dd)]) for dd in range(2)])
        prg = jnp.stack([jnp.stack([_pad_lanes(gdn_dt_bias[l, dd], 8 + 4 * dd, 32),
                                    _pad_lanes(a_gdn[dd], 8 + 4 * dd, 32)], axis=1) for dd in range(2)])
        gs = [_gdn_call(cv3, p3, smt3, pcg, prg, lc, dd, bool(dd)).reshape(n, GW) for dd in range(2)]

        ycat = _finish_call(hs + rs + ss + gs, p, cv, ret_norm_w[l][None, :],
                            jnp.repeat(ssd_d[l], SSD_HEAD_DIM)[None, :], ssd_norm_w[l][None, :],
                            jnp.tile(gdn_norm_w[l], GDN_HEADS)[None, :])

        rw = jnp.concatenate([router_w[l], jnp.zeros((d, N_EXPERTS), f32)], axis=1)
        rw_hi = rw.astype(bf16)
        rw_lo = (rw - rw_hi.astype(f32)).astype(bf16)
        x1, u_pk, logits_t = _outproj_call(ycat, w_out[l].astype(bf16), xall, g1, ln1_w[l][None, :],
                                           ln1_b[l][None, :], sc2, sh2, rw_hi, rw_lo, alpha)
        idx, wt_tm, rank, cnt = _route_rank_call(logits_t, router_bias[l][:, None])
        nt = _num_expert_tiles(TOP_K * n)
        te, n_used, last_tile, pstart = _expert_layout(cnt[:, 0].astype(jnp.int32), nt)
        pos, xs = _dispatch_call(last_tile, n_used, u_pk, idx, rank, pstart, nt)
        ys = _sorted_experts_call(te, n_used, xs, exp_w1, exp_w3, exp_w2, l)
        xall = _gather_combine_call(pos, wt_tm, u_pk, x1, g2, ln2_w[l][None, :], ln2_b[l][None, :],
                                    sh_w1[l].astype(bf16), sh_w3[l].astype(bf16), sh_w2[l].astype(bf16),
                                    ys, alpha)

    return xall.reshape(nb, seq, d)[:, lc:, :]
```
